```python
import functools
import jax, jax.numpy as jnp
from jax import lax
import numpy as np

D_MODEL = 1024
BATCH = 2
SEQ = 8192
DEPTH = 4
DEC_BATCH = 128
DEC_SEQ = 1
PAST_LEN = 8192
PAGE_SIZE = 128

BRANCH_WIDTH = 512
N_BRANCH = 3
CHUNK = 128
A_GROUPS = 4
A_GROUP_DIM = BRANCH_WIDTH // A_GROUPS
MLA_HEADS = 8
Q_LORA = 256
KV_LORA = 256
NOPE_DIM = 64
ROPE_DIM = 32
V_DIM = 64
LATENT_DIM = KV_LORA + ROPE_DIM
ROPE_THETA = 10000.0
Q_BLOCK = 128
CONV_W = 31
N_MEM = 256
MEM_HEADS = 4
MEM_HEAD_DIM = 64
MEM_WIDTH = MEM_HEADS * MEM_HEAD_DIM
D_FF = 4 * D_MODEL
EPS = 1e-6

IN_A = 2 * BRANCH_WIDTH
IN_B = Q_LORA + KV_LORA + ROPE_DIM
IN_C = 2 * BRANCH_WIDTH
IN_G = N_BRANCH * D_MODEL
IN_WIDTH = IN_A + IN_B + IN_C + IN_G

kernel_name = "gated_parallel_hybrid_decoder_step"


def rms_norm(x, g):
    xf = x.astype(jnp.float32)
    y = xf * lax.rsqrt(jnp.mean(xf * xf, axis=-1, keepdims=True) + EPS)
    return (y * g.astype(jnp.float32)).astype(x.dtype)


def layer_norm(x, g, b):
    xf = x.astype(jnp.float32)
    mu = jnp.mean(xf, axis=-1, keepdims=True)
    var = jnp.mean(jnp.square(xf - mu), axis=-1, keepdims=True)
    y = (xf - mu) * lax.rsqrt(var + EPS)
    return (y * g.astype(jnp.float32) + b.astype(jnp.float32)).astype(x.dtype)


def rope(x, pos):
    half = ROPE_DIM // 2
    inv = 1.0 / (ROPE_THETA ** (jnp.arange(half, dtype=jnp.float32) / half))
    ang = pos.astype(jnp.float32)[:, None] * inv[None, :]
    cos = jnp.cos(ang)[None, :, None, :]
    sin = jnp.sin(ang)[None, :, None, :]
    xf = x.astype(jnp.float32)
    x1, x2 = xf[..., :half], xf[..., half:]
    return jnp.concatenate([x1 * cos - x2 * sin, x2 * cos + x1 * sin], axis=-1).astype(x.dtype)


def chunk_spatial_mix(v, w_s, b_s):
    bsz, t, _ = v.shape
    n_chunks = -(-t // CHUNK)
    pad = n_chunks * CHUNK - t
    vp = jnp.pad(v, ((0, 0), (0, pad), (0, 0))).reshape(bsz, n_chunks, CHUNK, A_GROUPS, A_GROUP_DIM)
    w = w_s * jnp.tril(jnp.ones((CHUNK, CHUNK), w_s.dtype))
    z = jnp.einsum('gts,bnsgc->bntgc', w, vp) + b_s.T[None, None, :, :, None]
    return z.reshape(bsz, n_chunks * CHUNK, BRANCH_WIDTH)[:, :t]


def causal_depthwise_conv(x, prefix, w, b):
    xp = jnp.concatenate([prefix, x], axis=1)
    y = lax.conv_general_dilated(xp, w[:, None, :], window_strides=(1,), padding='VALID',
                                 dimension_numbers=('NWC', 'WIO', 'NWC'),
                                 feature_group_count=BRANCH_WIDTH)
    return y + b, xp[:, -(CONV_W - 1):]


def mla_attend_prompt(q_nope, q_rope, latent, w_ukv):
    bsz, s = latent.shape[:2]
    c_kv, k_rope = latent[..., :KV_LORA], latent[..., KV_LORA:]
    kv = jnp.einsum('bsl,lhd->bshd', c_kv, w_ukv.reshape(KV_LORA, MLA_HEADS, NOPE_DIM + V_DIM))
    k_nope, v = kv[..., :NOPE_DIM], kv[..., NOPE_DIM:]
    nb = s // Q_BLOCK
    qn = q_nope.reshape(bsz, nb, Q_BLOCK, MLA_HEADS, NOPE_DIM).swapaxes(0, 1)
    qr = q_rope.reshape(bsz, nb, Q_BLOCK, MLA_HEADS, ROPE_DIM).swapaxes(0, 1)
    k_pos = jnp.arange(s)

    def block(args):
        qn_b, qr_b, i = args
        sc = (jnp.einsum('bqhn,bkhn->bhqk', qn_b, k_nope)
              + jnp.einsum('bqhr,bkr->bhqk', qr_b, k_rope)).astype(jnp.float32)
        q_pos = i * Q_BLOCK + jnp.arange(Q_BLOCK)
        sc = jnp.where(k_pos[None, :] <= q_pos[:, None], sc, -jnp.inf)
        p = jax.nn.softmax(sc, axis=-1).astype(v.dtype)
        return jnp.einsum('bhqk,bkhv->bqhv', p, v)

    o = lax.map(block, (qn, qr, jnp.arange(nb)))
    return o.swapaxes(0, 1).reshape(bsz, s, MLA_HEADS * V_DIM)


def mla_attend_sample(q_nope, q_rope, latent_new, w_ukv, latent_past):
    bsz, t = latent_new.shape[:2]
    p_len = latent_past.shape[1]
    w = w_ukv.reshape(KV_LORA, MLA_HEADS, NOPE_DIM + V_DIM)
    w_uk, w_uv = w[..., :NOPE_DIM], w[..., NOPE_DIM:]
    q_lat = jnp.einsum('bthn,lhn->bthl', q_nope, w_uk)
    q_full = jnp.concatenate([q_lat, q_rope], axis=-1)
    sc_past = jnp.einsum('bthd,bkd->bhtk', q_full, latent_past).astype(jnp.float32)
    sc_new = jnp.einsum('bthd,bkd->bhtk', q_full, latent_new).astype(jnp.float32)
    causal = jnp.arange(t)[None, :] <= jnp.arange(t)[:, None]
    sc_new = jnp.where(causal, sc_new, -jnp.inf)
    p = jax.nn.softmax(jnp.concatenate([sc_past, sc_new], axis=-1), axis=-1).astype(latent_new.dtype)
    o_full = (jnp.einsum('bhtk,bkd->bthd', p[..., :p_len], latent_past)
              + jnp.einsum('bhtk,bkd->bthd', p[..., p_len:], latent_new))
    o = jnp.einsum('bthl,lhv->bthv', o_full[..., :KV_LORA], w_uv)
    return o.reshape(bsz, t, MLA_HEADS * V_DIM)


def memory_kv(mem, g, w_mk, w_mv):
    bsz = mem.shape[0]
    m = rms_norm(mem, g)
    k = (m @ w_mk).reshape(bsz, N_MEM, MEM_HEADS, MEM_HEAD_DIM)
    v = (m @ w_mv).reshape(bsz, N_MEM, MEM_HEADS, MEM_HEAD_DIM)
    return k, v


def memory_attend(h, k, v, w_mq, w_mo):
    bsz, t, _ = h.shape
    q = (h @ w_mq).reshape(bsz, t, MEM_HEADS, MEM_HEAD_DIM) * (MEM_HEAD_DIM ** -0.5)
    sc = jnp.einsum('bthd,bmhd->bhtm', q, k).astype(jnp.float32)
    p = jax.nn.softmax(sc, axis=-1).astype(v.dtype)
    o = jnp.einsum('bhtm,bmhd->bthd', p, v).reshape(bsz, t, MEM_WIDTH)
    return o @ w_mo


def decoder_layer(x, pos, conv_prefix, mem_k, mem_v, attend, p):
    bsz, t, _ = x.shape
    h = rms_norm(x, p['norms'][0])
    z = h @ p['w_in']
    za = z[..., :IN_A]
    zb = z[..., IN_A:IN_A + IN_B]
    zc = z[..., IN_A + IN_B:IN_A + IN_B + IN_C]
    zg = z[..., IN_A + IN_B + IN_C:]
    u = jax.nn.gelu(za[..., :BRANCH_WIDTH])
    v_n = rms_norm(jax.nn.gelu(za[..., BRANCH_WIDTH:]), p['a_v_norm'])
    out_a = u * chunk_spatial_mix(v_n, p['a_w_s'], p['a_b_s'])
    c_q = rms_norm(zb[..., :Q_LORA], p['q_norm'])
    q = (c_q @ p['w_uq']).reshape(bsz, t, MLA_HEADS, NOPE_DIM + ROPE_DIM) * ((NOPE_DIM + ROPE_DIM) ** -0.5)
    q_nope = q[..., :NOPE_DIM]
    q_rope = rope(q[..., NOPE_DIM:], pos)
    c_kv = rms_norm(zb[..., Q_LORA:Q_LORA + KV_LORA], p['kv_norm'])
    k_rope = rope(zb[..., Q_LORA + KV_LORA:][:, :, None, :], pos)[:, :, 0]
    latent = jnp.concatenate([c_kv, k_rope], axis=-1)
    out_b = attend(q_nope, q_rope, latent, p['w_ukv'])
    glu = zc[..., :BRANCH_WIDTH] * jax.nn.sigmoid(zc[..., BRANCH_WIDTH:])
    conv, conv_state = causal_depthwise_conv(glu, conv_prefix, p['conv_w'], p['conv_b'])
    out_c = jax.nn.silu(layer_norm(conv, p['conv_ln_g'], p['conv_ln_b']))
    branches = jnp.stack([out_a, out_b, out_c], axis=2)
    proj = jnp.einsum('btkc,kcd->btkd', branches, p['w_branch'])
    gates = jax.nn.sigmoid(zg.reshape(bsz, t, N_BRANCH, D_MODEL))
    merged = jnp.sum(gates * proj, axis=2)
    x = x + rms_norm(merged @ p['w_out'], p['norms'][1])
    h = rms_norm(x, p['norms'][2])
    x = x + rms_norm(memory_attend(h, mem_k, mem_v, p['w_mq'], p['w_mo']), p['norms'][3])
    h = rms_norm(x, p['norms'][4])
    f = jnp.square(jax.nn.relu(h @ p['w_ff1'])) @ p['w_ff2']
    x = x + rms_norm(f, p['norms'][5])
    return x, latent, conv_state, v_n


def setup_inputs(seed: int = 0) -> dict:
    key = jax.random.key(seed)
    ks = jax.random.split(key, 32)
    f32 = jnp.float32
    n_pages = PAST_LEN // PAGE_SIZE
    n_pool = (DEC_BATCH * n_pages * 5) // 4

    def nrm(k, shape, scale):
        return jax.random.normal(k, shape, f32) * scale

    def gain(k, shape):
        return 1.0 + 0.05 * jax.random.normal(k, shape, f32)

    page_table = jax.random.permutation(ks[7], n_pool)[:DEC_BATCH * n_pages]
    page_table = page_table.reshape(DEC_BATCH, n_pages).astype(jnp.int32)
    return {
        "x_prompt": nrm(ks[0], (BATCH, SEQ, D_MODEL), 1.0),
        "x_sample": nrm(ks[1], (DEC_BATCH, DEC_SEQ, D_MODEL), 1.0),
        "mem_prompt": nrm(ks[2], (BATCH, N_MEM, D_MODEL), 1.0),
        "cache_latent": nrm(ks[3], (DEPTH, n_pool, PAGE_SIZE, LATENT_DIM), 1.0),
        "cache_conv": nrm(ks[4], (DEPTH, DEC_BATCH, CONV_W - 1, BRANCH_WIDTH), 0.5),
        "cache_mem_k": nrm(ks[5], (DEPTH, DEC_BATCH, N_MEM, MEM_HEADS, MEM_HEAD_DIM), 1.0),
        "cache_mem_v": nrm(ks[6], (DEPTH, DEC_BATCH, N_MEM, MEM_HEADS, MEM_HEAD_DIM), 1.0),
        "page_table": page_table,
        "norm_gains": gain(ks[8], (DEPTH, 6, D_MODEL)),
        "mem_norm": gain(ks[9], (DEPTH, D_MODEL)),
        "w_in": nrm(ks[10], (DEPTH, D_MODEL, IN_WIDTH), D_MODEL ** -0.5),
        "a_v_norm": gain(ks[11], (DEPTH, BRANCH_WIDTH)),
        "a_w_s": nrm(ks[12], (DEPTH, A_GROUPS, CHUNK, CHUNK), CHUNK ** -0.5),
        "a_b_s": nrm(ks[13], (DEPTH, A_GROUPS, CHUNK), 0.02),
        "q_norm": gain(ks[14], (DEPTH, Q_LORA)),
        "w_uq": nrm(ks[15], (DEPTH, Q_LORA, MLA_HEADS * (NOPE_DIM + ROPE_DIM)), Q_LORA ** -0.5),
        "kv_norm": gain(ks[16], (DEPTH, KV_LORA)),
        "w_ukv": nrm(ks[17], (DEPTH, KV_LORA, MLA_HEADS * (NOPE_DIM + V_DIM)), KV_LORA ** -0.5),
        "conv_w": nrm(ks[18], (DEPTH, CONV_W, BRANCH_WIDTH), CONV_W ** -0.5),
        "conv_b": nrm(ks[19], (DEPTH, BRANCH_WIDTH), 0.02),
        "conv_ln_g": gain(ks[20], (DEPTH, BRANCH_WIDTH)),
        "conv_ln_b": nrm(ks[21], (DEPTH, BRANCH_WIDTH), 0.02),
        "w_branch": nrm(ks[22], (DEPTH, N_BRANCH, BRANCH_WIDTH, D_MODEL), BRANCH_WIDTH ** -0.5),
        "w_out": nrm(ks[23], (DEPTH, D_MODEL, D_MODEL), D_MODEL ** -0.5),
        "w_mq": nrm(ks[24], (DEPTH, D_MODEL, MEM_WIDTH), D_MODEL ** -0.5),
        "w_mk": nrm(ks[25], (DEPTH, D_MODEL, MEM_WIDTH), D_MODEL ** -0.5),
        "w_mv": nrm(ks[26], (DEPTH, D_MODEL, MEM_WIDTH), D_MODEL ** -0.5),
        "w_mo": nrm(ks[27], (DEPTH, MEM_WIDTH, D_MODEL), MEM_WIDTH ** -0.5),
        "w_ff1": nrm(ks[28], (DEPTH, D_MODEL, D_FF), D_MODEL ** -0.5),
        "w_ff2": nrm(ks[29], (DEPTH, D_FF, D_MODEL), D_FF ** -0.5),
    }


def reference(x_prompt, x_sample, mem_prompt, cache_latent, cache_conv, cache_mem_k, cache_mem_v,
              page_table, norm_gains, mem_norm, w_in, a_v_norm, a_w_s, a_b_s, q_norm, w_uq, kv_norm,
              w_ukv, conv_w, conv_b, conv_ln_g, conv_ln_b, w_branch, w_out, w_mq, w_mk, w_mv, w_mo,
              w_ff1, w_ff2):
    n_pages = page_table.shape[1]
    past_len = n_pages * PAGE_SIZE
    dec_b = x_sample.shape[0]
    pos_p = jnp.arange(x_prompt.shape[1], dtype=jnp.int32)
    pos_s = past_len + jnp.arange(x_sample.shape[1], dtype=jnp.int32)
    conv_zero = jnp.zeros((x_prompt.shape[0], CONV_W - 1, BRANCH_WIDTH), x_prompt.dtype)

    xp, xs = x_prompt, x_sample
    lat_p_l, lat_s_l, conv_p_l, conv_s_l, v_s_l, mk_p_l, mv_p_l = [], [], [], [], [], [], []
    for l in range(DEPTH):
        p = {
            'norms': norm_gains[l], 'w_in': w_in[l], 'a_v_norm': a_v_norm[l], 'a_w_s': a_w_s[l],
            'a_b_s': a_b_s[l], 'q_norm': q_norm[l], 'w_uq': w_uq[l], 'kv_norm': kv_norm[l],
            'w_ukv': w_ukv[l], 'conv_w': conv_w[l], 'conv_b': conv_b[l], 'conv_ln_g': conv_ln_g[l],
            'conv_ln_b': conv_ln_b[l], 'w_branch': w_branch[l], 'w_out': w_out[l], 'w_mq': w_mq[l],
            'w_mo': w_mo[l], 'w_ff1': w_ff1[l], 'w_ff2': w_ff2[l],
        }
        mk_p, mv_p = memory_kv(mem_prompt, mem_norm[l], w_mk[l], w_mv[l])
        xp, lat_p, conv_p, _ = decoder_layer(xp, pos_p, conv_zero, mk_p, mv_p, mla_attend_prompt, p)
        past = cache_latent[l][page_table].reshape(dec_b, past_len, LATENT_DIM)
        attend_s = functools.partial(mla_attend_sample, latent_past=past)
        xs, lat_s, conv_s, v_s = decoder_layer(xs, pos_s, cache_conv[l], cache_mem_k[l], cache_mem_v[l],
                                               attend_s, p)
        lat_p_l.append(lat_p); lat_s_l.append(lat_s)
        conv_p_l.append(conv_p); conv_s_l.append(conv_s)
        v_s_l.append(v_s); mk_p_l.append(mk_p); mv_p_l.append(mv_p)

    new_latent_prompt = jnp.stack(lat_p_l)
    new_latent_sample = jnp.stack(lat_s_l)
    new_conv_prompt = jnp.stack(conv_p_l)
    new_conv_sample = jnp.stack(conv_s_l)
    new_chunk_v_sample = jnp.stack(v_s_l)
    new_mem_k_prompt = jnp.stack(mk_p_l)
    new_mem_v_prompt = jnp.stack(mv_p_l)
    return (xp, xs, new_latent_prompt, new_latent_sample, new_conv_prompt, new_conv_sample,
            new_chunk_v_sample, new_mem_k_prompt, new_mem_v_prompt)
```

```python
import functools

import jax
import jax.numpy as jnp
import numpy as np
from jax import lax
from jax.experimental import pallas as pl
from jax.experimental.pallas import tpu as pltpu

F32 = jnp.float32
BF16 = jnp.bfloat16

D_MODEL = 1024
BRANCH = 512
N_BRANCH = 3
CHUNK = 128
A_GROUPS = 4
MLA_HEADS = 8
Q_LORA = 256
KV_LORA = 256
NOPE = 64
ROPE = 32
V_DIM = 64
LATENT = KV_LORA + ROPE
ROPE_THETA = 10000.0
CONV_W = 31
N_MEM = 256
MEM_HEADS = 4
MEM_HEAD_DIM = 64
MEM_WIDTH = MEM_HEADS * MEM_HEAD_DIM
D_FF = 4 * D_MODEL
EPS = 1e-6
PAGE = 128

HEAD_BLOCK = 128
QK_WIDTH = MLA_HEADS * HEAD_BLOCK
Q_SCALE = (NOPE + ROPE) ** -0.5
MEM_SCALE = MEM_HEAD_DIM ** -0.5

COL_A = 0
COL_B = COL_A + 2 * BRANCH
COL_C = COL_B + Q_LORA + KV_LORA + 2 * HEAD_BLOCK
COL_G = COL_C + 2 * BRANCH
IN_EXT = COL_G + N_BRANCH * D_MODEL

CONV_HALO = 32
VMEM_LIMIT = 56 * 1024 * 1024


def _params(*sem):
    return pltpu.CompilerParams(dimension_semantics=sem, vmem_limit_bytes=VMEM_LIMIT)


def _const_spec(shape):
    n = len(shape)
    return pl.BlockSpec(shape, lambda *_: (0,) * n, pipeline_mode=pl.Buffered(1))


def _dot(a, b):
    return jnp.dot(a, b, preferred_element_type=F32)


def _dot_nt(a, b):
    return lax.dot_general(a, b, (((1,), (1,)), ((), ())), preferred_element_type=F32)


def _rms(x, g):
    return x * lax.rsqrt(jnp.mean(x * x, axis=-1, keepdims=True) + EPS) * g


def _gelu(x):
    c = np.sqrt(2.0 / np.pi).astype(np.float32)
    return x * (0.5 * (1.0 + jnp.tanh(c * (x + 0.044715 * (x * x * x)))))


def _sigmoid(x):
    return 1.0 / (1.0 + jnp.exp(-x))


def _silu_layer_norm(x, g, b):
    mu = jnp.mean(x, axis=-1, keepdims=True)
    xc = x - mu
    var = jnp.mean(xc * xc, axis=-1, keepdims=True)
    y = xc * lax.rsqrt(var + EPS) * g + b
    return y * _sigmoid(y)


def _mix_front(x_ref, g0_ref, w_in_ref, cos_ref, sin_ref, avn_ref, qn_ref, kvn_ref, w_uq_ref,
               q_ref, lat_ref, gates_ref):
    h = _rms(x_ref[...], g0_ref[...]).astype(BF16)

    za = _dot(h, w_in_ref[:, COL_A:COL_B])
    u = _gelu(za[:, :BRANCH])
    v_n = _rms(_gelu(za[:, BRANCH:]), avn_ref[...])

    zb = _dot(h, w_in_ref[:, COL_B:COL_C])
    c_q = _rms(zb[:, :Q_LORA], qn_ref[...])
    c_kv = _rms(zb[:, Q_LORA:Q_LORA + KV_LORA], kvn_ref[...])
    cos = cos_ref[...]
    sin = sin_ref[...]
    o = Q_LORA + KV_LORA
    k_rope = zb[:, o:o + HEAD_BLOCK] * cos + zb[:, o + HEAD_BLOCK:o + 2 * HEAD_BLOCK] * sin

    qq = _dot(c_q.astype(BF16), w_uq_ref[...]) * Q_SCALE
    for hd in range(MLA_HEADS):
        lo = hd * HEAD_BLOCK
        q_h = qq[:, lo:lo + HEAD_BLOCK] * cos + qq[:, QK_WIDTH + lo:QK_WIDTH + lo + HEAD_BLOCK] * sin
        q_ref[:, lo:lo + HEAD_BLOCK] = q_h.astype(q_ref.dtype)

    lat_ref[:, :KV_LORA] = c_kv
    lat_ref[:, KV_LORA:] = k_rope[:, :ROPE]

    zc = _dot(h, w_in_ref[:, COL_C:COL_G])
    glu = zc[:, :BRANCH] * _sigmoid(zc[:, BRANCH:])

    for k in range(N_BRANCH):
        lo = COL_G + k * D_MODEL
        gates_ref[:, k * D_MODEL:(k + 1) * D_MODEL] = _sigmoid(_dot(h, w_in_ref[:, lo:lo + D_MODEL]))
    return u, v_n, c_kv, k_rope, glu


def _mix_prompt_kernel(x_ref, cos_ref, sin_ref, g0_ref, w_in_ref, avn_ref, ws_ref, bs_ref, qn_ref,
                       kvn_ref, w_uq_ref, w_kv_ref, cw_ref, cb_ref, lng_ref, lnb_ref,
                       outa_ref, q_ref, lat_ref, k_ref, v_ref, outc_ref, gates_ref, tail_ref,
                       xp_ref, *, tm, tiles_per_seq):
    u, v_n, c_kv, k_rope, glu = _mix_front(x_ref, g0_ref, w_in_ref, cos_ref, sin_ref, avn_ref,
                                           qn_ref, kvn_ref, w_uq_ref, q_ref, lat_ref, gates_ref)

    row = lax.broadcasted_iota(jnp.int32, (CHUNK, CHUNK), 0)
    col = lax.broadcasted_iota(jnp.int32, (CHUNK, CHUNK), 1)
    causal = col <= row
    for g in range(A_GROUPS):
        w_g = jnp.where(causal, ws_ref[g], 0.0).astype(BF16)
        b_g = bs_ref[:, g:g + 1]
        for c in range(tm // CHUNK):
            rs = slice(c * CHUNK, (c + 1) * CHUNK)
            cs = slice(g * CHUNK, (g + 1) * CHUNK)
            mixed = _dot(w_g, v_n[rs, cs].astype(BF16)) + b_g
            outa_ref[rs, cs] = (u[rs, cs] * mixed).astype(outa_ref.dtype)

    kv = _dot(c_kv.astype(BF16), w_kv_ref[...])
    for hd in range(MLA_HEADS):
        lo = hd * HEAD_BLOCK
        k_ref[:, lo:lo + HEAD_BLOCK] = (kv[:, lo:lo + HEAD_BLOCK] + k_rope).astype(k_ref.dtype)
    v_ref[...] = kv[:, QK_WIDTH:].astype(v_ref.dtype)

    first = (pl.program_id(0) % tiles_per_seq) == 0

    @pl.when(first)
    def _():
        xp_ref[:CONV_HALO, :] = jnp.zeros((CONV_HALO, BRANCH), F32)

    @pl.when(jnp.logical_not(first))
    def _():
        xp_ref[:CONV_HALO, :] = xp_ref[tm:tm + CONV_HALO, :]

    xp_ref[CONV_HALO:, :] = glu
    acc = jnp.zeros((tm, BRANCH), F32) + cb_ref[...]
    off = CONV_HALO - (CONV_W - 1)
    for j in range(CONV_W):
        acc = acc + cw_ref[j:j + 1, :] * xp_ref[off + j:off + j + tm, :]
    outc_ref[...] = _silu_layer_norm(acc, lng_ref[...], lnb_ref[...]).astype(outc_ref.dtype)
    tail_ref[...] = xp_ref[tm:tm + CONV_HALO, :]


def _mix_sample_kernel(x_ref, cos_ref, sin_ref, g0_ref, w_in_ref, avn_ref, ws0_ref, bs0_ref, qn_ref,
                       kvn_ref, w_uq_ref, cw_ref, cb_ref, lng_ref, lnb_ref, conv_ref,
                       outa_ref, q_ref, lat_ref, outc_ref, gates_ref, vn_ref, glu_ref):
    u, v_n, _, _, glu = _mix_front(x_ref, g0_ref, w_in_ref, cos_ref, sin_ref, avn_ref,
                                   qn_ref, kvn_ref, w_uq_ref, q_ref, lat_ref, gates_ref)
    vn_ref[...] = v_n
    glu_ref[...] = glu
    outa_ref[...] = (u * (ws0_ref[...] * v_n + bs0_ref[...])).astype(outa_ref.dtype)
    acc = cb_ref[...] + cw_ref[CONV_W - 1:CONV_W, :] * glu
    for j in range(CONV_W - 1):
        acc = acc + cw_ref[j:j + 1, :] * conv_ref[j]
    outc_ref[...] = _silu_layer_norm(acc, lng_ref[...], lnb_ref[...]).astype(outc_ref.dtype)


def _mix_prompt(x, cos, sin, lw, *, tm):
    rows = x.shape[0]
    seq = cos.shape[0]
    tiles_per_seq = seq // tm
    n_seq = rows // seq
    grid = (rows // tm,)
    row_spec = lambda w: pl.BlockSpec((tm, w), lambda i: (i, 0))
    pos_spec = pl.BlockSpec((tm, HEAD_BLOCK), lambda i: (i % tiles_per_seq, 0))
    in_specs = [
        row_spec(D_MODEL), pos_spec, pos_spec,
        _const_spec((1, D_MODEL)), _const_spec((D_MODEL, IN_EXT)), _const_spec((1, BRANCH)),
        _const_spec((A_GROUPS, CHUNK, CHUNK)), _const_spec((CHUNK, A_GROUPS)),
        _const_spec((1, Q_LORA)), _const_spec((1, KV_LORA)),
        _const_spec((Q_LORA, 2 * QK_WIDTH)), _const_spec((KV_LORA, QK_WIDTH + BRANCH)),
        _const_spec((CONV_W, BRANCH)), _const_spec((1, BRANCH)), _const_spec((1, BRANCH)),
        _const_spec((1, BRANCH)),
    ]
    out_shape = (
        jax.ShapeDtypeStruct((rows, BRANCH), BF16),
        jax.ShapeDtypeStruct((rows, QK_WIDTH), BF16),
        jax.ShapeDtypeStruct((rows, LATENT), F32),
        jax.ShapeDtypeStruct((rows, QK_WIDTH), BF16),
        jax.ShapeDtypeStruct((rows, BRANCH), BF16),
        jax.ShapeDtypeStruct((rows, BRANCH), BF16),
        jax.ShapeDtypeStruct((rows, N_BRANCH * D_MODEL), F32),
        jax.ShapeDtypeStruct((n_seq, CONV_HALO, BRANCH), F32),
    )
    out_specs = (
        row_spec(BRANCH), row_spec(QK_WIDTH), row_spec(LATENT), row_spec(QK_WIDTH), row_spec(BRANCH),
        row_spec(BRANCH), row_spec(N_BRANCH * D_MODEL),
        pl.BlockSpec((None, CONV_HALO, BRANCH), lambda i: (i // tiles_per_seq, 0, 0)),
    )
    return pl.pallas_call(
        functools.partial(_mix_prompt_kernel, tm=tm, tiles_per_seq=tiles_per_seq),
        grid=grid, in_specs=in_specs, out_specs=out_specs, out_shape=out_shape,
        scratch_shapes=[pltpu.VMEM((tm + CONV_HALO, BRANCH), F32)],
        compiler_params=_params("arbitrary"), name="mix_prompt",
    )(x, cos, sin, lw["g0"], lw["w_in"], lw["a_v_norm"], lw["a_w_s"], lw["a_b_s_t"], lw["q_norm"],
      lw["kv_norm"], lw["w_uq"], lw["w_kv"], lw["conv_w"], lw["conv_b"], lw["conv_ln_g"],
      lw["conv_ln_b"])


def _mix_sample(x, cos, sin, lw, conv_t):
    rows = x.shape[0]
    full = lambda shape: pl.BlockSpec(shape, lambda i: (0,) * len(shape))
    in_specs = [
        full((rows, D_MODEL)), full((rows, HEAD_BLOCK)), full((rows, HEAD_BLOCK)),
        full((1, D_MODEL)), _const_spec((D_MODEL, IN_EXT)), full((1, BRANCH)),
        full((1, BRANCH)), full((1, BRANCH)), full((1, Q_LORA)), full((1, KV_LORA)),
        full((Q_LORA, 2 * QK_WIDTH)), full((CONV_W, BRANCH)), full((1, BRANCH)), full((1, BRANCH)),
        full((1, BRANCH)), full((CONV_W - 1, rows, BRANCH)),
    ]
    out_shape = (
        jax.ShapeDtypeStruct((rows, BRANCH), BF16),
        jax.ShapeDtypeStruct((rows, QK_WIDTH), F32),
        jax.ShapeDtypeStruct((rows, LATENT), F32),
        jax.ShapeDtypeStruct((rows, BRANCH), BF16),
        jax.ShapeDtypeStruct((rows, N_BRANCH * D_MODEL), F32),
        jax.ShapeDtypeStruct((rows, BRANCH), F32),
        jax.ShapeDtypeStruct((rows, BRANCH), F32),
    )
    out_specs = tuple(full(s.shape) for s in out_shape)
    return pl.pallas_call(
        _mix_sample_kernel, grid=(1,), in_specs=in_specs, out_specs=out_specs, out_shape=out_shape,
        compiler_params=_params("arbitrary"), name="mix_sample",
    )(x, cos, sin, lw["g0"], lw["w_in"], lw["a_v_norm"], lw["a_w_s0"], lw["a_b_s0"], lw["q_norm"],
      lw["kv_norm"], lw["w_uq"], lw["conv_w"], lw["conv_b"], lw["conv_ln_g"], lw["conv_ln_b"], conv_t)


def _attn_prompt_kernel(qi_ref, ki_ref, q_ref, k_ref, v_ref, o_ref, m_ref, l_ref, acc_ref, *, t):
    step = pl.program_id(1)
    qi = qi_ref[step]
    ki = ki_ref[step]
    lane = lax.broadcasted_iota(jnp.int32, (t, HEAD_BLOCK), 1)
    low_half = lane < V_DIM

    @pl.when(ki == 0)
    def _():
        m_ref[...] = jnp.full(m_ref.shape, -jnp.inf, F32)
        l_ref[...] = jnp.zeros(l_ref.shape, F32)
        acc_ref[...] = jnp.zeros(acc_ref.shape, F32)

    def update(masked):
        if masked:
            row = lax.broadcasted_iota(jnp.int32, (t, t), 0)
            col = lax.broadcasted_iota(jnp.int32, (t, t), 1)
            keep = col <= row
        for pair in range(MLA_HEADS // 2):
            v_pair = v_ref[:, pair * HEAD_BLOCK:(pair + 1) * HEAD_BLOCK]
            alphas, pvs = [], []
            for hd in (2 * pair, 2 * pair + 1):
                lo = hd * HEAD_BLOCK
                s = _dot_nt(q_ref[:, lo:lo + HEAD_BLOCK], k_ref[:, lo:lo + HEAD_BLOCK])
                if masked:
                    s = jnp.where(keep, s, -jnp.inf)
                m_old = m_ref[hd]
                m_new = jnp.maximum(m_old, jnp.max(s, axis=-1, keepdims=True))
                alpha = jnp.exp(m_old - m_new)
                p = jnp.exp(s - m_new)
                l_ref[hd] = alpha * l_ref[hd] + jnp.sum(p, axis=-1, keepdims=True)
                m_ref[hd] = m_new
                alphas.append(alpha)
                pvs.append(_dot(p.astype(BF16), v_pair))
            alpha2 = jnp.where(low_half, alphas[0], alphas[1])
            acc_ref[pair] = acc_ref[pair] * alpha2 + jnp.where(low_half, pvs[0], pvs[1])

    @pl.when(ki < qi)
    def _():
        update(False)

    @pl.when(ki == qi)
    def _():
        update(True)
        for pair in range(MLA_HEADS // 2):
            l2 = jnp.where(low_half, l_ref[2 * pair], l_ref[2 * pair + 1])
            o_ref[:, pair * HEAD_BLOCK:(pair + 1) * HEAD_BLOCK] = (acc_ref[pair] / l2).astype(o_ref.dtype)


def _attn_prompt(q, k, v, *, seq, t):
    rows = q.shape[0]
    n_seq = rows // seq
    nb = seq // t
    qi_tab = np.array([i for i in range(nb) for _ in range(i + 1)], np.int32)
    ki_tab = np.array([j for i in range(nb) for j in range(i + 1)], np.int32)
    grid_spec = pltpu.PrefetchScalarGridSpec(
        num_scalar_prefetch=2,
        grid=(n_seq, len(qi_tab)),
        in_specs=[
            pl.BlockSpec((t, QK_WIDTH), lambda b, s, qi, ki: (b * nb + qi[s], 0)),
            pl.BlockSpec((t, QK_WIDTH), lambda b, s, qi, ki: (b * nb + ki[s], 0)),
            pl.BlockSpec((t, BRANCH), lambda b, s, qi, ki: (b * nb + ki[s], 0)),
        ],
        out_specs=pl.BlockSpec((t, BRANCH), lambda b, s, qi, ki: (b * nb + qi[s], 0)),
        scratch_shapes=[
            pltpu.VMEM((MLA_HEADS, t, 1), F32),
            pltpu.VMEM((MLA_HEADS, t, 1), F32),
            pltpu.VMEM((MLA_HEADS // 2, t, HEAD_BLOCK), F32),
        ],
    )
    return pl.pallas_call(
        functools.partial(_attn_prompt_kernel, t=t),
        grid_spec=grid_spec, out_shape=jax.ShapeDtypeStruct((rows, BRANCH), BF16),
        compiler_params=_params("arbitrary", "arbitrary"), name="attn_prompt",
    )(jnp.asarray(qi_tab), jnp.asarray(ki_tab), q, k, v)


def _q_latent_kernel(q_ref, w_uk_ref, qlat_ref, qrope_ref):
    lane = lax.broadcasted_iota(jnp.int32, (q_ref.shape[0], HEAD_BLOCK), 1)
    for hd in range(MLA_HEADS):
        q_h = q_ref[:, hd * HEAD_BLOCK:(hd + 1) * HEAD_BLOCK]
        qlat_ref[hd] = _dot(q_h.astype(BF16), w_uk_ref[hd])
        qrope_ref[hd] = jnp.where(lane < ROPE, q_h, 0.0)


def _q_latent(q, w_uk):
    rows = q.shape[0]
    full = lambda shape: pl.BlockSpec(shape, lambda i: (0,) * len(shape))
    return pl.pallas_call(
        _q_latent_kernel, grid=(1,),
        in_specs=[full((rows, QK_WIDTH)), full((MLA_HEADS, HEAD_BLOCK, KV_LORA))],
        out_specs=(full((MLA_HEADS, rows, KV_LORA)), full((MLA_HEADS, rows, HEAD_BLOCK))),
        out_shape=(jax.ShapeDtypeStruct((MLA_HEADS, rows, KV_LORA), F32),
                   jax.ShapeDtypeStruct((MLA_HEADS, rows, HEAD_BLOCK), F32)),
        compiler_params=_params("arbitrary"), name="q_latent",
    )(q, w_uk)


def _attn_sample_kernel(pt_ref, qlat_ref, qrope_ref, latn_ref, cache_ref, o_ref, buf_ref, sem_ref, *,
                        layer, n_pages, key_chunk):
    b = pl.program_id(0)
    nb = pl.num_programs(0)
    past = n_pages * PAGE

    def page_copy(sample, page, slot):
        return pltpu.make_async_copy(
            cache_ref.at[layer, pt_ref[sample, page]],
            buf_ref.at[slot, pl.ds(page * PAGE, PAGE), :],
            sem_ref.at[slot])

    def start_all(sample, slot):
        def body(page, carry):
            page_copy(sample, page, slot).start()
            return carry
        lax.fori_loop(0, n_pages, body, 0)

    def wait_all(sample, slot):
        def body(page, carry):
            page_copy(sample, page, slot).wait()
            return carry
        lax.fori_loop(0, n_pages, body, 0)

    slot = b % 2

    @pl.when(b == 0)
    def _():
        start_all(0, 0)

    @pl.when(b + 1 < nb)
    def _():
        start_all(b + 1, 1 - slot)

    wait_all(b, slot)

    q_lat = qlat_ref[...].astype(BF16)
    q_rope = qrope_ref[:, :ROPE].astype(BF16)
    lat_new = latn_ref[...].astype(BF16).astype(F32)
    s_new = (jnp.sum(q_lat.astype(F32) * lat_new[:, :KV_LORA], axis=-1, keepdims=True)
             + jnp.sum(q_rope.astype(F32) * lat_new[:, KV_LORA:], axis=-1, keepdims=True))

    n_chunks = past // key_chunk
    scores = []
    for c in range(n_chunks):
        rows = buf_ref[slot, c * key_chunk:(c + 1) * key_chunk, :].astype(BF16)
        scores.append(_dot_nt(q_lat, rows[:, :KV_LORA]) + _dot_nt(q_rope, rows[:, KV_LORA:]))
    m = s_new
    for s in scores:
        m = jnp.maximum(m, jnp.max(s, axis=-1, keepdims=True))
    exps = [jnp.exp(s - m) for s in scores]
    e_new = jnp.exp(s_new - m)
    denom = e_new
    for e in exps:
        denom = denom + jnp.sum(e, axis=-1, keepdims=True)
    p_new = (e_new / denom).astype(BF16).astype(F32)
    out = p_new * lat_new[:, :KV_LORA]
    for c in range(n_chunks):
        p = (exps[c] / denom).astype(BF16)
        rows = buf_ref[slot, c * key_chunk:(c + 1) * key_chunk, :KV_LORA].astype(BF16)
        out = out + _dot(p, rows)
    o_ref[...] = out


def _attn_sample(page_table, q_lat, q_rope, lat_new, cache_latent, *, layer):
    n_s, n_pages = page_table.shape
    past = n_pages * PAGE
    key_chunk = 1024 if past % 1024 == 0 else PAGE
    grid_spec = pltpu.PrefetchScalarGridSpec(
        num_scalar_prefetch=1,
        grid=(n_s,),
        in_specs=[
            pl.BlockSpec((None, MLA_HEADS, KV_LORA), lambda b, pt: (b, 0, 0)),
            pl.BlockSpec((None, MLA_HEADS, HEAD_BLOCK), lambda b, pt: (b, 0, 0)),
            pl.BlockSpec((None, 1, LATENT), lambda b, pt: (b, 0, 0)),
            pl.BlockSpec(memory_space=pl.ANY),
        ],
        out_specs=pl.BlockSpec((None, MLA_HEADS, KV_LORA), lambda b, pt: (b, 0, 0)),
        scratch_shapes=[
            pltpu.VMEM((2, past, LATENT), F32),
            pltpu.SemaphoreType.DMA((2,)),
        ],
    )
    return pl.pallas_call(
        functools.partial(_attn_sample_kernel, layer=layer, n_pages=n_pages, key_chunk=key_chunk),
        grid_spec=grid_spec, out_shape=jax.ShapeDtypeStruct((n_s, MLA_HEADS, KV_LORA), F32),
        compiler_params=_params("arbitrary"), name="attn_sample",
    )(page_table, q_lat, q_rope, lat_new, cache_latent)


def _v_up_kernel(o_ref, w_uv_ref, out_ref):
    acc = _dot(o_ref[0].astype(BF16), w_uv_ref[0])
    for hd in range(1, MLA_HEADS):
        acc = acc + _dot(o_ref[hd].astype(BF16), w_uv_ref[hd])
    out_ref[...] = acc.astype(out_ref.dtype)


def _v_up(o_lat, w_uv):
    rows = o_lat.shape[1]
    full = lambda shape: pl.BlockSpec(shape, lambda i: (0,) * len(shape))
    return pl.pallas_call(
        _v_up_kernel, grid=(1,),
        in_specs=[full((MLA_HEADS, rows, KV_LORA)), full((MLA_HEADS, KV_LORA, BRANCH))],
        out_specs=full((rows, BRANCH)),
        out_shape=jax.ShapeDtypeStruct((rows, BRANCH), BF16),
        compiler_params=_params("arbitrary"), name="v_up",
    )(o_lat, w_uv)


def _merge_kernel(x_ref, a_ref, b_ref, c_ref, gates_ref, wb_ref, wo_ref, g1_ref, g2_ref, wmq_ref,
                  x1_ref, qm_ref):
    merged = None
    for k, br_ref in enumerate((a_ref, b_ref, c_ref)):
        term = gates_ref[:, k * D_MODEL:(k + 1) * D_MODEL] * _dot(br_ref[...], wb_ref[k])
        merged = term if merged is None else merged + term
    y = _dot(merged.astype(BF16), wo_ref[...])
    x1 = x_ref[...] + _rms(y, g1_ref[...])
    x1_ref[...] = x1
    h = _rms(x1, g2_ref[...]).astype(BF16)
    qm_ref[...] = _dot(h, wmq_ref[...]) * MEM_SCALE


def _merge(x, out_a, out_b, out_c, gates, lw, *, tm):
    rows = x.shape[0]
    row_spec = lambda w: pl.BlockSpec((tm, w), lambda i: (i, 0))
    return pl.pallas_call(
        _merge_kernel, grid=(rows // tm,),
        in_specs=[row_spec(D_MODEL), row_spec(BRANCH), row_spec(BRANCH), row_spec(BRANCH),
                  row_spec(N_BRANCH * D_MODEL), _const_spec((N_BRANCH, BRANCH, D_MODEL)),
                  _const_spec((D_MODEL, D_MODEL)), _const_spec((1, D_MODEL)), _const_spec((1, D_MODEL)),
                  _const_spec((D_MODEL, MEM_WIDTH))],
        out_specs=(row_spec(D_MODEL), row_spec(MEM_WIDTH)),
        out_shape=(jax.ShapeDtypeStruct((rows, D_MODEL), F32),
                   jax.ShapeDtypeStruct((rows, MEM_WIDTH), F32)),
        compiler_params=_params("arbitrary"), name="merge",
    )(x, out_a, out_b, out_c, gates, lw["w_branch"], lw["w_out"], lw["g1"], lw["g2"], lw["w_mq"])


def _memory_kv_kernel(mem_ref, g_ref, wk_ref, wv_ref, k_ref, v_ref):
    m = _rms(mem_ref[...], g_ref[...]).astype(BF16)
    k_ref[...] = _dot(m, wk_ref[...])
    v_ref[...] = _dot(m, wv_ref[...])


def _memory_kv(mem, g, w_mk, w_mv):
    rows = mem.shape[0]
    full = lambda shape: pl.BlockSpec(shape, lambda i: (0,) * len(shape))
    return pl.pallas_call(
        _memory_kv_kernel, grid=(1,),
        in_specs=[full((rows, D_MODEL)), full((1, D_MODEL)), full((D_MODEL, MEM_WIDTH)),
                  full((D_MODEL, MEM_WIDTH))],
        out_specs=(full((rows, MEM_WIDTH)), full((rows, MEM_WIDTH))),
        out_shape=(jax.ShapeDtypeStruct((rows, MEM_WIDTH), F32),) * 2,
        compiler_params=_params("arbitrary"), name="memory_kv",
    )(mem, g, w_mk, w_mv)


def _head_lane_mask(rows):
    lane = lax.broadcasted_iota(jnp.int32, (rows, MEM_WIDTH), 1)
    return [(lane >= hd * MEM_HEAD_DIM) & (lane < (hd + 1) * MEM_HEAD_DIM) for hd in range(MEM_HEADS)]


def _mem_attn_prompt_kernel(qm_ref, k_ref, v_ref, o_ref):
    qm = qm_ref[...]
    k = k_ref[...].astype(BF16)
    v = v_ref[...].astype(BF16)
    masks = _head_lane_mask(qm.shape[0])
    out = jnp.zeros(qm.shape, F32)
    for hd in range(MEM_HEADS):
        s = _dot_nt(jnp.where(masks[hd], qm, 0.0).astype(BF16), k)
        e = jnp.exp(s - jnp.max(s, axis=-1, keepdims=True))
        p = (e / jnp.sum(e, axis=-1, keepdims=True)).astype(BF16)
        out = out + jnp.where(masks[hd], _dot(p, v), 0.0)
    o_ref[...] = out.astype(o_ref.dtype)


def _mem_attn_prompt(qm, mem_k, mem_v, *, seq, tm):
    rows = qm.shape[0]
    tiles_per_seq = seq // tm
    kv_spec = pl.BlockSpec((None, N_MEM, MEM_WIDTH), lambda i: (i // tiles_per_seq, 0, 0))
    return pl.pallas_call(
        _mem_attn_prompt_kernel, grid=(rows // tm,),
        in_specs=[pl.BlockSpec((tm, MEM_WIDTH), lambda i: (i, 0)), kv_spec, kv_spec],
        out_specs=pl.BlockSpec((tm, MEM_WIDTH), lambda i: (i, 0)),
        out_shape=jax.ShapeDtypeStruct((rows, MEM_WIDTH), BF16),
        compiler_params=_params("arbitrary"), name="mem_attn_prompt",
    )(qm, mem_k, mem_v)


def _mem_attn_sample_kernel(qm_ref, k_ref, v_ref, o_ref, *, group):
    sub = lax.broadcasted_iota(jnp.int32, (8, MEM_WIDTH), 0)
    lane = lax.broadcasted_iota(jnp.int32, (8, MEM_WIDTH), 1)
    own = (lane >= sub * MEM_HEAD_DIM) & (lane < (sub + 1) * MEM_HEAD_DIM)
    for i in range(group):
        q_rows = jnp.where(own, qm_ref[i:i + 1, :], 0.0).astype(BF16)
        s = _dot_nt(q_rows, k_ref[i].astype(BF16))
        e = jnp.exp(s - jnp.max(s, axis=-1, keepdims=True))
        p = (e / jnp.sum(e, axis=-1, keepdims=True)).astype(BF16)
        o_all = _dot(p, v_ref[i].astype(BF16))
        o_ref[i:i + 1, :] = jnp.sum(jnp.where(own, o_all, 0.0), axis=0, keepdims=True).astype(o_ref.dtype)


def _mem_attn_sample(qm, cache_k, cache_v, *, layer, group):
    rows = qm.shape[0]
    kv_spec = pl.BlockSpec((None, group, N_MEM, MEM_WIDTH), lambda i: (layer, i, 0, 0))
    return pl.pallas_call(
        functools.partial(_mem_attn_sample_kernel, group=group), grid=(rows // group,),
        in_specs=[pl.BlockSpec((group, MEM_WIDTH), lambda i: (i, 0)), kv_spec, kv_spec],
        out_specs=pl.BlockSpec((group, MEM_WIDTH), lambda i: (i, 0)),
        out_shape=jax.ShapeDtypeStruct((rows, MEM_WIDTH), F32),
        compiler_params=_params("arbitrary"), name="mem_attn_sample",
    )(qm, cache_k, cache_v)


def _mlp_kernel(x1_ref, o_ref, wmo_ref, g3_ref, g4_ref, w1_ref, w2_ref, g5_ref, x3_ref, *, ff_chunk):
    x2 = x1_ref[...] + _rms(_dot(o_ref[...].astype(BF16), wmo_ref[...]), g3_ref[...])
    h = _rms(x2, g4_ref[...]).astype(BF16)
    f = None
    for c in range(D_FF // ff_chunk):
        cs = slice(c * ff_chunk, (c + 1) * ff_chunk)
        a = jnp.maximum(_dot(h, w1_ref[:, cs]), 0.0)
        part = _dot((a * a).astype(BF16), w2_ref[cs, :])
        f = part if f is None else f + part
    x3_ref[...] = x2 + _rms(f, g5_ref[...])


def _mlp(x1, o, lw, *, tm):
    rows = x1.shape[0]
    row_spec = lambda w: pl.BlockSpec((tm, w), lambda i: (i, 0))
    return pl.pallas_call(
        functools.partial(_mlp_kernel, ff_chunk=1024), grid=(rows // tm,),
        in_specs=[row_spec(D_MODEL), row_spec(MEM_WIDTH), _const_spec((MEM_WIDTH, D_MODEL)),
                  _const_spec((1, D_MODEL)), _const_spec((1, D_MODEL)), _const_spec((D_MODEL, D_FF)),
                  _const_spec((D_FF, D_MODEL)), _const_spec((1, D_MODEL))],
        out_specs=row_spec(D_MODEL),
        out_shape=jax.ShapeDtypeStruct((rows, D_MODEL), F32),
        compiler_params=_params("arbitrary"), name="mlp",
    )(x1, o, lw["w_mo"], lw["g3"], lw["g4"], lw["w_ff1"], lw["w_ff2"], lw["g5"])


def _rotate_half_cols(w):
    half = ROPE // 2
    return jnp.concatenate([-w[..., half:], w[..., :half]], axis=-1)


def _rope_tables(pos):
    half = ROPE // 2
    inv = 1.0 / (ROPE_THETA ** (jnp.arange(half, dtype=F32) / half))
    ang = pos.astype(F32)[:, None] * inv[None, :]
    cos, sin = jnp.cos(ang), jnp.sin(ang)
    n = pos.shape[0]
    cos_blk = jnp.concatenate([cos, cos, jnp.ones((n, HEAD_BLOCK - ROPE), F32)], axis=1)
    sin_blk = jnp.concatenate([sin, sin, jnp.zeros((n, HEAD_BLOCK - ROPE), F32)], axis=1)
    return cos_blk, sin_blk


def _layer_weights(l, norm_gains, w_in, a_v_norm, a_w_s, a_b_s, q_norm, w_uq, kv_norm, w_ukv, conv_w,
                   conv_b, conv_ln_g, conv_ln_b, w_branch, w_out, w_mq, w_mo, w_ff1, w_ff2):
    row = lambda v: v.reshape(1, -1)
    wi = w_in[l]
    o_b = 2 * BRANCH
    kr = wi[:, o_b + Q_LORA + KV_LORA:o_b + Q_LORA + KV_LORA + ROPE]
    pad = jnp.zeros((D_MODEL, HEAD_BLOCK - ROPE), F32)
    o_c = o_b + Q_LORA + KV_LORA + ROPE
    w_in_ext = jnp.concatenate([
        wi[:, :o_b + Q_LORA + KV_LORA], kr, pad, _rotate_half_cols(kr), pad, wi[:, o_c:]], axis=1)

    uq = w_uq[l].reshape(Q_LORA, MLA_HEADS, NOPE + ROPE)
    uq_nope, uq_rope = uq[..., :NOPE], uq[..., NOPE:]
    z = lambda n: jnp.zeros((Q_LORA, MLA_HEADS, n), F32)
    uq_pad = jnp.concatenate([uq_rope, uq_nope, z(HEAD_BLOCK - ROPE - NOPE)], axis=-1)
    uq_rot = jnp.concatenate([_rotate_half_cols(uq_rope), z(HEAD_BLOCK - ROPE)], axis=-1)
    w_uq_ext = jnp.concatenate([uq_pad.reshape(Q_LORA, QK_WIDTH), uq_rot.reshape(Q_LORA, QK_WIDTH)], axis=1)

    ukv = w_ukv[l].reshape(KV_LORA, MLA_HEADS, NOPE + V_DIM)
    uk, uv = ukv[..., :NOPE], ukv[..., NOPE:]
    zk = lambda n: jnp.zeros((KV_LORA, MLA_HEADS, n), F32)
    kn_pad = jnp.concatenate([zk(ROPE), uk, zk(HEAD_BLOCK - ROPE - NOPE)], axis=-1)
    w_kv = jnp.concatenate([kn_pad.reshape(KV_LORA, QK_WIDTH), uv.reshape(KV_LORA, BRANCH)], axis=1)
    uk_t = jnp.transpose(uk, (1, 2, 0))
    w_uk_abs = jnp.concatenate([jnp.zeros((MLA_HEADS, ROPE, KV_LORA), F32), uk_t,
                                jnp.zeros((MLA_HEADS, HEAD_BLOCK - ROPE - NOPE, KV_LORA), F32)], axis=1)
    eye = jnp.eye(MLA_HEADS, dtype=F32)
    w_uv_abs = (jnp.transpose(uv, (1, 0, 2))[:, :, None, :] * eye[:, None, :, None]).reshape(
        MLA_HEADS, KV_LORA, BRANCH)

    g = norm_gains[l]
    return {
        "g0": row(g[0]), "g1": row(g[1]), "g2": row(g[2]), "g3": row(g[3]), "g4": row(g[4]), "g5": row(g[5]),
        "w_in": w_in_ext.astype(BF16),
        "a_v_norm": row(a_v_norm[l]), "a_w_s": a_w_s[l], "a_b_s_t": a_b_s[l].T,
        "a_w_s0": row(jnp.repeat(a_w_s[l][:, 0, 0], CHUNK)), "a_b_s0": row(jnp.repeat(a_b_s[l][:, 0], CHUNK)),
        "q_norm": row(q_norm[l]), "kv_norm": row(kv_norm[l]),
        "w_uq": w_uq_ext.astype(BF16), "w_kv": w_kv.astype(BF16),
        "w_uk_abs": w_uk_abs.astype(BF16), "w_uv_abs": w_uv_abs.astype(BF16),
        "conv_w": conv_w[l], "conv_b": row(conv_b[l]), "conv_ln_g": row(conv_ln_g[l]),
        "conv_ln_b": row(conv_ln_b[l]),
        "w_branch": w_branch[l].astype(BF16), "w_out": w_out[l].astype(BF16),
        "w_mq": w_mq[l].astype(BF16), "w_mo": w_mo[l].astype(BF16),
        "w_ff1": w_ff1[l].astype(BF16), "w_ff2": w_ff2[l].astype(BF16),
    }


def _row_tile(rows, want):
    t = min(rows, want)
    while rows % t:
        t //= 2
    return t


def kernel(x_prompt, x_sample, mem_prompt, cache_latent, cache_conv, cache_mem_k, cache_mem_v, page_table,
           norm_gains, mem_norm, w_in, a_v_norm, a_w_s, a_b_s, q_norm, w_uq, kv_norm, w_ukv, conv_w, conv_b,
           conv_ln_g, conv_ln_b, w_branch, w_out, w_mq, w_mk, w_mv, w_mo, w_ff1, w_ff2):
    depth = w_in.shape[0]
    n_seq, seq, _ = x_prompt.shape
    n_s, t_s, _ = x_sample.shape
    assert t_s == 1 and seq % CHUNK == 0 and seq >= CONV_HALO
    n_pages = page_table.shape[1]
    past = n_pages * PAGE

    tm_mix = _row_tile(seq, 512)
    tm_row = _row_tile(seq, 512)
    t_attn = _row_tile(seq, 512)
    tm_s = n_s
    mem_group = _row_tile(n_s, 8)

    cos_p, sin_p = _rope_tables(jnp.arange(seq, dtype=jnp.int32))
    cos_s, sin_s = _rope_tables(jnp.full((n_s,), past, jnp.int32))

    xp = x_prompt.reshape(n_seq * seq, D_MODEL)
    xs = x_sample.reshape(n_s, D_MODEL)
    mem = mem_prompt.reshape(n_seq * N_MEM, D_MODEL)
    cache_k = cache_mem_k.reshape(depth, n_s, N_MEM, MEM_WIDTH)
    cache_v = cache_mem_v.reshape(depth, n_s, N_MEM, MEM_WIDTH)

    lat_p_l, lat_s_l, conv_p_l, conv_s_l, v_s_l, mk_p_l, mv_p_l = [], [], [], [], [], [], []
    for l in range(depth):
        lw = _layer_weights(l, norm_gains, w_in, a_v_norm, a_w_s, a_b_s, q_norm, w_uq, kv_norm, w_ukv,
                            conv_w, conv_b, conv_ln_g, conv_ln_b, w_branch, w_out, w_mq, w_mo, w_ff1, w_ff2)

        mk, mv = _memory_kv(mem, mem_norm[l].reshape(1, -1), w_mk[l].astype(BF16), w_mv[l].astype(BF16))
        out_a, q, lat, k, v, out_c, gates, tail = _mix_prompt(xp, cos_p, sin_p, lw, tm=tm_mix)
        out_b = _attn_prompt(q, k, v, seq=seq, t=t_attn)
        x1, qm = _merge(xp, out_a, out_b, out_c, gates, lw, tm=tm_row)
        o_mem = _mem_attn_prompt(qm, mk.reshape(n_seq, N_MEM, MEM_WIDTH), mv.reshape(n_seq, N_MEM, MEM_WIDTH),
                                 seq=seq, tm=tm_row)
        xp = _mlp(x1, o_mem, lw, tm=tm_row)
        lat_p_l.append(lat.reshape(n_seq, seq, LATENT))
        conv_p_l.append(tail[:, CONV_HALO - (CONV_W - 1):, :])
        mk_p_l.append(mk.reshape(n_seq, N_MEM, MEM_HEADS, MEM_HEAD_DIM))
        mv_p_l.append(mv.reshape(n_seq, N_MEM, MEM_HEADS, MEM_HEAD_DIM))

        conv_t = jnp.transpose(cache_conv[l], (1, 0, 2))
        out_a, q, lat, out_c, gates, v_n, glu = _mix_sample(xs, cos_s, sin_s, lw, conv_t)
        q_lat, q_rope = _q_latent(q, lw["w_uk_abs"])
        o_lat = _attn_sample(page_table, jnp.transpose(q_lat, (1, 0, 2)), jnp.transpose(q_rope, (1, 0, 2)),
                             lat.reshape(n_s, 1, LATENT), cache_latent, layer=l)
        out_b = _v_up(jnp.transpose(o_lat, (1, 0, 2)), lw["w_uv_abs"])
        x1, qm = _merge(xs, out_a, out_b, out_c, gates, lw, tm=tm_s)
        o_mem = _mem_attn_sample(qm, cache_k, cache_v, layer=l, group=mem_group)
        xs = _mlp(x1, o_mem, lw, tm=tm_s)
        lat_s_l.append(lat.reshape(n_s, 1, LATENT))
        conv_s_l.append(jnp.concatenate([cache_conv[l][:, 1:, :], glu[:, None, :]], axis=1))
        v_s_l.append(v_n.reshape(n_s, 1, BRANCH))

    return (xp.reshape(n_seq, seq, D_MODEL), xs.reshape(n_s, 1, D_MODEL),
            jnp.stack(lat_p_l), jnp.stack(lat_s_l), jnp.stack(conv_p_l), jnp.stack(conv_s_l),
            jnp.stack(v_s_l), jnp.stack(mk_p_l), jnp.stack(mv_p_l))
```

```python
import functools

import jax
import jax.numpy as jnp
import numpy as np
from jax import lax
from jax.experimental import pallas as pl
from jax.experimental.pallas import tpu as pltpu

F32 = jnp.float32
BF16 = jnp.bfloat16

D_MODEL = 1024
BRANCH = 512
N_BRANCH = 3
CHUNK = 128
A_GROUPS = 4
MLA_HEADS = 8
Q_LORA = 256
KV_LORA = 256
NOPE = 64
ROPE = 32
V_DIM = 64
LATENT = KV_LORA + ROPE
ROPE_THETA = 10000.0
CONV_W = 31
N_MEM = 256
MEM_HEADS = 4
MEM_HEAD_DIM = 64
MEM_WIDTH = MEM_HEADS * MEM_HEAD_DIM
D_FF = 4 * D_MODEL
EPS = 1e-6
PAGE = 128

HEAD_BLOCK = 128
QK_WIDTH = MLA_HEADS * HEAD_BLOCK
Q_SCALE = (NOPE + ROPE) ** -0.5
MEM_SCALE = MEM_HEAD_DIM ** -0.5
LOG2_E = float(np.log2(np.e))

COL_A = 0
COL_B = COL_A + 2 * BRANCH
COL_C = COL_B + Q_LORA + KV_LORA + 2 * HEAD_BLOCK
COL_G = COL_C + 2 * BRANCH
IN_EXT = COL_G + N_BRANCH * D_MODEL

CONV_HALO = 32
VMEM_LIMIT = 56 * 1024 * 1024


def _params(*sem):
    return pltpu.CompilerParams(dimension_semantics=sem, vmem_limit_bytes=VMEM_LIMIT)


def _const_spec(shape):
    n = len(shape)
    return pl.BlockSpec(shape, lambda *_: (0,) * n, pipeline_mode=pl.Buffered(1))


def _dot(a, b):
    return jnp.dot(a, b, preferred_element_type=F32)


def _dot_nt(a, b):
    return lax.dot_general(a, b, (((1,), (1,)), ((), ())), preferred_element_type=F32)


def _rms(x, g):
    return x * lax.rsqrt(jnp.mean(x * x, axis=-1, keepdims=True) + EPS) * g


def _gelu(x):
    c = np.sqrt(2.0 / np.pi).astype(np.float32)
    return x * (0.5 * (1.0 + jnp.tanh(c * (x + 0.044715 * (x * x * x)))))


def _sigmoid(x):
    return 1.0 / (1.0 + jnp.exp(-x))


def _silu_layer_norm(x, g, b):
    mu = jnp.mean(x, axis=-1, keepdims=True)
    xc = x - mu
    var = jnp.mean(xc * xc, axis=-1, keepdims=True)
    y = xc * lax.rsqrt(var + EPS) * g + b
    return y * _sigmoid(y)


def _mix_front(x_ref, g0_ref, w_in_ref, cos_ref, sin_ref, avn_ref, qn_ref, kvn_ref, w_uq_ref,
               q_ref, lat_ref, gates_ref, q_scale):
    h = _rms(x_ref[...], g0_ref[...]).astype(BF16)

    za = _dot(h, w_in_ref[:, COL_A:COL_B])
    u = _gelu(za[:, :BRANCH])
    v_n = _rms(_gelu(za[:, BRANCH:]), avn_ref[...])

    zb = _dot(h, w_in_ref[:, COL_B:COL_C])
    c_q = _rms(zb[:, :Q_LORA], qn_ref[...])
    c_kv = _rms(zb[:, Q_LORA:Q_LORA + KV_LORA], kvn_ref[...])
    cos = cos_ref[...]
    sin = sin_ref[...]
    o = Q_LORA + KV_LORA
    k_rope = zb[:, o:o + HEAD_BLOCK] * cos + zb[:, o + HEAD_BLOCK:o + 2 * HEAD_BLOCK] * sin

    qq = _dot(c_q.astype(BF16), w_uq_ref[...]) * q_scale
    for hd in range(MLA_HEADS):
        lo = hd * HEAD_BLOCK
        q_h = qq[:, lo:lo + HEAD_BLOCK] * cos + qq[:, QK_WIDTH + lo:QK_WIDTH + lo + HEAD_BLOCK] * sin
        q_ref[:, lo:lo + HEAD_BLOCK] = q_h.astype(q_ref.dtype)

    lat_ref[:, :KV_LORA] = c_kv
    lat_ref[:, KV_LORA:] = k_rope[:, :ROPE]

    zc = _dot(h, w_in_ref[:, COL_C:COL_G])
    glu = zc[:, :BRANCH] * _sigmoid(zc[:, BRANCH:])

    for k in range(N_BRANCH):
        lo = COL_G + k * D_MODEL
        gates_ref[:, k * D_MODEL:(k + 1) * D_MODEL] = _sigmoid(_dot(h, w_in_ref[:, lo:lo + D_MODEL]))
    return u, v_n, c_kv, k_rope, glu


def _mix_prompt_kernel(x_ref, cos_ref, sin_ref, g0_ref, w_in_ref, avn_ref, ws_ref, bs_ref, qn_ref,
                       kvn_ref, w_uq_ref, w_k_ref, w_vt_ref, cw_ref, cb_ref, lng_ref, lnb_ref,
                       outa_ref, q_ref, lat_ref, k_ref, vt_ref, outc_ref, gates_ref, tail_ref,
                       xp_ref, sh_ref, *, tm, tiles_per_seq):
    u, v_n, c_kv, k_rope, glu = _mix_front(x_ref, g0_ref, w_in_ref, cos_ref, sin_ref, avn_ref,
                                           qn_ref, kvn_ref, w_uq_ref, q_ref, lat_ref, gates_ref,
                                           Q_SCALE * LOG2_E)

    row = lax.broadcasted_iota(jnp.int32, (CHUNK, CHUNK), 0)
    col = lax.broadcasted_iota(jnp.int32, (CHUNK, CHUNK), 1)
    causal = col <= row
    for g in range(A_GROUPS):
        w_g = jnp.where(causal, ws_ref[g], 0.0).astype(BF16)
        b_g = bs_ref[:, g:g + 1]
        for c in range(tm // CHUNK):
            rs = slice(c * CHUNK, (c + 1) * CHUNK)
            cs = slice(g * CHUNK, (g + 1) * CHUNK)
            mixed = _dot(w_g, v_n[rs, cs].astype(BF16)) + b_g
            outa_ref[rs, cs] = (u[rs, cs] * mixed).astype(outa_ref.dtype)

    c_kv16 = c_kv.astype(BF16)
    k_nope = _dot(c_kv16, w_k_ref[...])
    for hd in range(MLA_HEADS):
        lo = hd * HEAD_BLOCK
        k_ref[:, lo:lo + HEAD_BLOCK] = (k_nope[:, lo:lo + HEAD_BLOCK] + k_rope).astype(k_ref.dtype)
    vt_ref[...] = _dot_nt(w_vt_ref[...], c_kv16).astype(vt_ref.dtype)

    first = (pl.program_id(0) % tiles_per_seq) == 0

    @pl.when(first)
    def _():
        xp_ref[:CONV_HALO, :] = jnp.zeros((CONV_HALO, BRANCH), F32)

    @pl.when(jnp.logical_not(first))
    def _():
        xp_ref[:CONV_HALO, :] = xp_ref[tm:tm + CONV_HALO, :]

    xp_ref[CONV_HALO:, :] = glu
    acc = jnp.zeros((tm, BRANCH), F32) + cb_ref[...]
    off = CONV_HALO - (CONV_W - 1)
    for phase in range(8):
        taps = [j for j in range(CONV_W) if (off + j) % 8 == phase]
        if not taps:
            continue
        base = min(off + j for j in taps)
        span = max(off + j for j in taps) - base + tm
        if phase == 0:
            src_ref, src_base = xp_ref, base
        else:
            sh_ref[:span, :] = xp_ref[base:base + span, :]
            src_ref, src_base = sh_ref, 0
        for j in taps:
            lo = src_base + off + j - base
            acc = acc + cw_ref[j:j + 1, :] * src_ref[lo:lo + tm, :]
    outc_ref[...] = _silu_layer_norm(acc, lng_ref[...], lnb_ref[...]).astype(outc_ref.dtype)
    tail_ref[...] = xp_ref[tm:tm + CONV_HALO, :]


def _mix_sample_kernel(x_ref, cos_ref, sin_ref, g0_ref, w_in_ref, avn_ref, ws0_ref, bs0_ref, qn_ref,
                       kvn_ref, w_uq_ref, cw_ref, cb_ref, lng_ref, lnb_ref, conv_ref,
                       outa_ref, q_ref, lat_ref, outc_ref, gates_ref, vn_ref, glu_ref):
    u, v_n, _, _, glu = _mix_front(x_ref, g0_ref, w_in_ref, cos_ref, sin_ref, avn_ref,
                                   qn_ref, kvn_ref, w_uq_ref, q_ref, lat_ref, gates_ref, Q_SCALE)
    vn_ref[...] = v_n
    glu_ref[...] = glu
    outa_ref[...] = (u * (ws0_ref[...] * v_n + bs0_ref[...])).astype(outa_ref.dtype)
    acc = cb_ref[...] + cw_ref[CONV_W - 1:CONV_W, :] * glu
    for j in range(CONV_W - 1):
        acc = acc + cw_ref[j:j + 1, :] * conv_ref[j]
    outc_ref[...] = _silu_layer_norm(acc, lng_ref[...], lnb_ref[...]).astype(outc_ref.dtype)


def _mix_prompt(x, cos, sin, lw, *, tm):
    rows = x.shape[0]
    seq = cos.shape[0]
    tiles_per_seq = seq // tm
    n_seq = rows // seq
    grid = (rows // tm,)
    row_spec = lambda w: pl.BlockSpec((tm, w), lambda i: (i, 0))
    pos_spec = pl.BlockSpec((tm, HEAD_BLOCK), lambda i: (i % tiles_per_seq, 0))
    in_specs = [
        row_spec(D_MODEL), pos_spec, pos_spec,
        _const_spec((1, D_MODEL)), _const_spec((D_MODEL, IN_EXT)), _const_spec((1, BRANCH)),
        _const_spec((A_GROUPS, CHUNK, CHUNK)), _const_spec((CHUNK, A_GROUPS)),
        _const_spec((1, Q_LORA)), _const_spec((1, KV_LORA)),
        _const_spec((Q_LORA, 2 * QK_WIDTH)), _const_spec((KV_LORA, QK_WIDTH)),
        _const_spec((BRANCH, KV_LORA)),
        _const_spec((CONV_W, BRANCH)), _const_spec((1, BRANCH)), _const_spec((1, BRANCH)),
        _const_spec((1, BRANCH)),
    ]
    out_shape = (
        jax.ShapeDtypeStruct((rows, BRANCH), BF16),
        jax.ShapeDtypeStruct((rows, QK_WIDTH), BF16),
        jax.ShapeDtypeStruct((rows, LATENT), F32),
        jax.ShapeDtypeStruct((rows, QK_WIDTH), BF16),
        jax.ShapeDtypeStruct((n_seq, BRANCH, seq), BF16),
        jax.ShapeDtypeStruct((rows, BRANCH), BF16),
        jax.ShapeDtypeStruct((rows, N_BRANCH * D_MODEL), F32),
        jax.ShapeDtypeStruct((n_seq, CONV_HALO, BRANCH), F32),
    )
    out_specs = (
        row_spec(BRANCH), row_spec(QK_WIDTH), row_spec(LATENT), row_spec(QK_WIDTH),
        pl.BlockSpec((None, BRANCH, tm), lambda i: (i // tiles_per_seq, 0, i % tiles_per_seq)),
        row_spec(BRANCH), row_spec(N_BRANCH * D_MODEL),
        pl.BlockSpec((None, CONV_HALO, BRANCH), lambda i: (i // tiles_per_seq, 0, 0)),
    )
    return pl.pallas_call(
        functools.partial(_mix_prompt_kernel, tm=tm, tiles_per_seq=tiles_per_seq),
        grid=grid, in_specs=in_specs, out_specs=out_specs, out_shape=out_shape,
        scratch_shapes=[pltpu.VMEM((tm + CONV_HALO, BRANCH), F32)] * 2,
        compiler_params=_params("arbitrary"), name="mix_prompt",
    )(x, cos, sin, lw["g0"], lw["w_in"], lw["a_v_norm"], lw["a_w_s"], lw["a_b_s_t"], lw["q_norm"],
      lw["kv_norm"], lw["w_uq"], lw["w_k"], lw["w_vt"], lw["conv_w"], lw["conv_b"], lw["conv_ln_g"],
      lw["conv_ln_b"])


def _mix_sample(x, cos, sin, lw, conv_t):
    rows = x.shape[0]
    full = lambda shape: pl.BlockSpec(shape, lambda i: (0,) * len(shape))
    in_specs = [
        full((rows, D_MODEL)), full((rows, HEAD_BLOCK)), full((rows, HEAD_BLOCK)),
        full((1, D_MODEL)), _const_spec((D_MODEL, IN_EXT)), full((1, BRANCH)),
        full((1, BRANCH)), full((1, BRANCH)), full((1, Q_LORA)), full((1, KV_LORA)),
        full((Q_LORA, 2 * QK_WIDTH)), full((CONV_W, BRANCH)), full((1, BRANCH)), full((1, BRANCH)),
        full((1, BRANCH)), full((CONV_W - 1, rows, BRANCH)),
    ]
    out_shape = (
        jax.ShapeDtypeStruct((rows, BRANCH), BF16),
        jax.ShapeDtypeStruct((rows, QK_WIDTH), F32),
        jax.ShapeDtypeStruct((rows, LATENT), F32),
        jax.ShapeDtypeStruct((rows, BRANCH), BF16),
        jax.ShapeDtypeStruct((rows, N_BRANCH * D_MODEL), F32),
        jax.ShapeDtypeStruct((rows, BRANCH), F32),
        jax.ShapeDtypeStruct((rows, BRANCH), F32),
    )
    out_specs = tuple(full(s.shape) for s in out_shape)
    return pl.pallas_call(
        _mix_sample_kernel, grid=(1,), in_specs=in_specs, out_specs=out_specs, out_shape=out_shape,
        compiler_params=_params("arbitrary"), name="mix_sample",
    )(x, cos, sin, lw["g0"], lw["w_in"], lw["a_v_norm"], lw["a_w_s0"], lw["a_b_s0"], lw["q_norm"],
      lw["kv_norm"], lw["w_uq"], lw["conv_w"], lw["conv_b"], lw["conv_ln_g"], lw["conv_ln_b"], conv_t)


def _attn_prompt_kernel(qi_ref, ki_ref, q_ref, k_ref, vt_ref, o_ref, m_ref, l_ref, acc_ref, *, t):
    step = pl.program_id(1)
    qi = qi_ref[step]
    ki = ki_ref[step]

    @pl.when(ki == 0)
    def _():
        m_ref[...] = jnp.full(m_ref.shape, -jnp.inf, F32)
        l_ref[...] = jnp.zeros(l_ref.shape, F32)
        acc_ref[...] = jnp.zeros(acc_ref.shape, F32)

    def update(masked):
        if masked:
            key = lax.broadcasted_iota(jnp.int32, (t, t), 0)
            qry = lax.broadcasted_iota(jnp.int32, (t, t), 1)
            keep = key <= qry
        def scores(hd):
            lo = hd * HEAD_BLOCK
            return _dot_nt(k_ref[:, lo:lo + HEAD_BLOCK], q_ref[:, lo:lo + HEAD_BLOCK])

        s_next = scores(0)
        for hd in range(MLA_HEADS):
            vs = slice(hd * V_DIM, (hd + 1) * V_DIM)
            s = s_next
            if hd + 1 < MLA_HEADS:
                s_next = scores(hd + 1)
            if masked:
                s = jnp.where(keep, s, -jnp.inf)
            m_old = m_ref[hd]
            m_new = jnp.maximum(m_old, jnp.max(s, axis=0, keepdims=True))
            alpha = jnp.exp2(m_old - m_new)
            p = jnp.exp2(s - m_new)
            l_ref[hd] = alpha * l_ref[hd] + jnp.sum(p.reshape(t // 8, 8, t), axis=0)
            m_ref[hd] = m_new
            acc_ref[vs, :] = acc_ref[vs, :] * alpha + _dot(vt_ref[vs, :], p.astype(BF16))

    @pl.when(ki < qi)
    def _():
        update(False)

    @pl.when(ki == qi)
    def _():
        update(True)
        for hd in range(MLA_HEADS):
            vs = slice(hd * V_DIM, (hd + 1) * V_DIM)
            acc_ref[vs, :] = acc_ref[vs, :] / jnp.sum(l_ref[hd], axis=0, keepdims=True)
        o_ref[...] = acc_ref[...].T.astype(o_ref.dtype)


def _attn_prompt(q, k, vt, *, seq, t):
    rows = q.shape[0]
    n_seq = rows // seq
    nb = seq // t
    qi_tab = np.array([i for i in range(nb) for _ in range(i + 1)], np.int32)
    ki_tab = np.array([j for i in range(nb) for j in range(i + 1)], np.int32)
    grid_spec = pltpu.PrefetchScalarGridSpec(
        num_scalar_prefetch=2,
        grid=(n_seq, len(qi_tab)),
        in_specs=[
            pl.BlockSpec((t, QK_WIDTH), lambda b, s, qi, ki: (b * nb + qi[s], 0)),
            pl.BlockSpec((t, QK_WIDTH), lambda b, s, qi, ki: (b * nb + ki[s], 0)),
            pl.BlockSpec((None, BRANCH, t), lambda b, s, qi, ki: (b, 0, ki[s])),
        ],
        out_specs=pl.BlockSpec((t, BRANCH), lambda b, s, qi, ki: (b * nb + qi[s], 0)),
        scratch_shapes=[
            pltpu.VMEM((MLA_HEADS, 1, t), F32),
            pltpu.VMEM((MLA_HEADS, 8, t), F32),
            pltpu.VMEM((BRANCH, t), F32),
        ],
    )
    return pl.pallas_call(
        functools.partial(_attn_prompt_kernel, t=t),
        grid_spec=grid_spec, out_shape=jax.ShapeDtypeStruct((rows, BRANCH), BF16),
        compiler_params=_params("arbitrary", "arbitrary"), name="attn_prompt",
    )(jnp.asarray(qi_tab), jnp.asarray(ki_tab), q, k, vt)


def _q_latent_kernel(q_ref, w_uk_ref, qlat_ref, qrope_ref):
    lane = lax.broadcasted_iota(jnp.int32, (q_ref.shape[0], HEAD_BLOCK), 1)
    for hd in range(MLA_HEADS):
        q_h = q_ref[:, hd * HEAD_BLOCK:(hd + 1) * HEAD_BLOCK]
        qlat_ref[hd] = _dot(q_h.astype(BF16), w_uk_ref[hd])
        qrope_ref[hd] = jnp.where(lane < ROPE, q_h, 0.0)


def _q_latent(q, w_uk):
    rows = q.shape[0]
    full = lambda shape: pl.BlockSpec(shape, lambda i: (0,) * len(shape))
    return pl.pallas_call(
        _q_latent_kernel, grid=(1,),
        in_specs=[full((rows, QK_WIDTH)), full((MLA_HEADS, HEAD_BLOCK, KV_LORA))],
        out_specs=(full((MLA_HEADS, rows, KV_LORA)), full((MLA_HEADS, rows, HEAD_BLOCK))),
        out_shape=(jax.ShapeDtypeStruct((MLA_HEADS, rows, KV_LORA), F32),
                   jax.ShapeDtypeStruct((MLA_HEADS, rows, HEAD_BLOCK), F32)),
        compiler_params=_params("arbitrary"), name="q_latent",
    )(q, w_uk)


def _attn_sample_kernel(pt_ref, qlat_ref, qrope_ref, latn_ref, cache_ref, o_ref, buf_ref, sem_ref, *,
                        layer, n_pages, key_chunk):
    b = pl.program_id(0)
    nb = pl.num_programs(0)
    past = n_pages * PAGE

    def page_copy(sample, page, slot):
        return pltpu.make_async_copy(
            cache_ref.at[layer, pt_ref[sample, page]],
            buf_ref.at[slot, :, pl.ds(pl.multiple_of(page * PAGE, PAGE), PAGE)],
            sem_ref.at[slot])

    def start_all(sample, slot):
        def body(page, carry):
            page_copy(sample, page, slot).start()
            return carry
        lax.fori_loop(0, n_pages, body, 0)

    def wait_all(sample, slot):
        def body(page, carry):
            page_copy(sample, page, slot).wait()
            return carry
        lax.fori_loop(0, n_pages, body, 0)

    slot = b % 2

    @pl.when(b == 0)
    def _():
        start_all(0, 0)

    @pl.when(b + 1 < nb)
    def _():
        start_all(b + 1, 1 - slot)

    wait_all(b, slot)

    q_lat = qlat_ref[...].astype(BF16)
    q_rope = qrope_ref[:, :ROPE].astype(BF16)
    lat_new = latn_ref[...].astype(BF16).astype(F32)
    s_new = (jnp.sum(q_lat.astype(F32) * lat_new[:, :KV_LORA], axis=-1, keepdims=True)
             + jnp.sum(q_rope.astype(F32) * lat_new[:, KV_LORA:], axis=-1, keepdims=True))

    n_chunks = past // key_chunk
    scores = []
    for c in range(n_chunks):
        cols = buf_ref[slot, :, c * key_chunk:(c + 1) * key_chunk].astype(BF16)
        scores.append(_dot(q_lat, cols[:KV_LORA, :]) + _dot(q_rope, cols[KV_LORA:, :]))
    m = s_new
    for s in scores:
        m = jnp.maximum(m, jnp.max(s, axis=-1, keepdims=True))
    exps = [jnp.exp(s - m) for s in scores]
    e_new = jnp.exp(s_new - m)
    denom = e_new
    for e in exps:
        denom = denom + jnp.sum(e, axis=-1, keepdims=True)
    p_new = (e_new / denom).astype(BF16).astype(F32)
    out = p_new * lat_new[:, :KV_LORA]
    for c in range(n_chunks):
        p = (exps[c] / denom).astype(BF16)
        cols = buf_ref[slot, :KV_LORA, c * key_chunk:(c + 1) * key_chunk].astype(BF16)
        out = out + _dot_nt(p, cols)
    o_ref[...] = out


def _attn_sample(page_table, q_lat, q_rope, lat_new, cache_latent, *, layer):
    n_s, n_pages = page_table.shape
    past = n_pages * PAGE
    key_chunk = 1024 if past % 1024 == 0 else PAGE
    grid_spec = pltpu.PrefetchScalarGridSpec(
        num_scalar_prefetch=1,
        grid=(n_s,),
        in_specs=[
            pl.BlockSpec((None, MLA_HEADS, KV_LORA), lambda b, pt: (b, 0, 0)),
            pl.BlockSpec((None, MLA_HEADS, HEAD_BLOCK), lambda b, pt: (b, 0, 0)),
            pl.BlockSpec((None, 1, LATENT), lambda b, pt: (b, 0, 0)),
            pl.BlockSpec(memory_space=pl.ANY),
        ],
        out_specs=pl.BlockSpec((None, MLA_HEADS, KV_LORA), lambda b, pt: (b, 0, 0)),
        scratch_shapes=[
            pltpu.VMEM((2, LATENT, past), F32),
            pltpu.SemaphoreType.DMA((2,)),
        ],
    )
    return pl.pallas_call(
        functools.partial(_attn_sample_kernel, layer=layer, n_pages=n_pages, key_chunk=key_chunk),
        grid_spec=grid_spec, out_shape=jax.ShapeDtypeStruct((n_s, MLA_HEADS, KV_LORA), F32),
        compiler_params=_params("arbitrary"), name="attn_sample",
    )(page_table, q_lat, q_rope, lat_new, cache_latent)


def _v_up_kernel(o_ref, w_uv_ref, out_ref):
    acc = _dot(o_ref[0].astype(BF16), w_uv_ref[0])
    for hd in range(1, MLA_HEADS):
        acc = acc + _dot(o_ref[hd].astype(BF16), w_uv_ref[hd])
    out_ref[...] = acc.astype(out_ref.dtype)


def _v_up(o_lat, w_uv):
    rows = o_lat.shape[1]
    full = lambda shape: pl.BlockSpec(shape, lambda i: (0,) * len(shape))
    return pl.pallas_call(
        _v_up_kernel, grid=(1,),
        in_specs=[full((MLA_HEADS, rows, KV_LORA)), full((MLA_HEADS, KV_LORA, BRANCH))],
        out_specs=full((rows, BRANCH)),
        out_shape=jax.ShapeDtypeStruct((rows, BRANCH), BF16),
        compiler_params=_params("arbitrary"), name="v_up",
    )(o_lat, w_uv)


def _merge_kernel(x_ref, a_ref, b_ref, c_ref, gates_ref, wb_ref, wo_ref, g1_ref, g2_ref, wmq_ref,
                  x1_ref, qm_ref):
    merged = None
    for k, br_ref in enumerate((a_ref, b_ref, c_ref)):
        term = gates_ref[:, k * D_MODEL:(k + 1) * D_MODEL] * _dot(br_ref[...], wb_ref[k])
        merged = term if merged is None else merged + term
    y = _dot(merged.astype(BF16), wo_ref[...])
    x1 = x_ref[...] + _rms(y, g1_ref[...])
    x1_ref[...] = x1
    h = _rms(x1, g2_ref[...]).astype(BF16)
    qm_ref[...] = _dot(h, wmq_ref[...]) * MEM_SCALE


def _merge(x, out_a, out_b, out_c, gates, lw, *, tm):
    rows = x.shape[0]
    row_spec = lambda w: pl.BlockSpec((tm, w), lambda i: (i, 0))
    return pl.pallas_call(
        _merge_kernel, grid=(rows // tm,),
        in_specs=[row_spec(D_MODEL), row_spec(BRANCH), row_spec(BRANCH), row_spec(BRANCH),
                  row_spec(N_BRANCH * D_MODEL), _const_spec((N_BRANCH, BRANCH, D_MODEL)),
                  _const_spec((D_MODEL, D_MODEL)), _const_spec((1, D_MODEL)), _const_spec((1, D_MODEL)),
                  _const_spec((D_MODEL, MEM_WIDTH))],
        out_specs=(row_spec(D_MODEL), row_spec(MEM_WIDTH)),
        out_shape=(jax.ShapeDtypeStruct((rows, D_MODEL), F32),
                   jax.ShapeDtypeStruct((rows, MEM_WIDTH), F32)),
        compiler_params=_params("arbitrary"), name="merge",
    )(x, out_a, out_b, out_c, gates, lw["w_branch"], lw["w_out"], lw["g1"], lw["g2"], lw["w_mq"])


def _memory_kv_kernel(mem_ref, g_ref, wk_ref, wv_ref, k_ref, v_ref):
    m = _rms(mem_ref[...], g_ref[...]).astype(BF16)
    k_ref[...] = _dot(m, wk_ref[...])
    v_ref[...] = _dot(m, wv_ref[...])


def _memory_kv(mem, g, w_mk, w_mv):
    rows = mem.shape[0]
    full = lambda shape: pl.BlockSpec(shape, lambda i: (0,) * len(shape))
    return pl.pallas_call(
        _memory_kv_kernel, grid=(1,),
        in_specs=[full((rows, D_MODEL)), full((1, D_MODEL)), full((D_MODEL, MEM_WIDTH)),
                  full((D_MODEL, MEM_WIDTH))],
        out_specs=(full((rows, MEM_WIDTH)), full((rows, MEM_WIDTH))),
        out_shape=(jax.ShapeDtypeStruct((rows, MEM_WIDTH), F32),) * 2,
        compiler_params=_params("arbitrary"), name="memory_kv",
    )(mem, g, w_mk, w_mv)


def _head_lane_mask(rows):
    lane = lax.broadcasted_iota(jnp.int32, (rows, MEM_WIDTH), 1)
    return [(lane >= hd * MEM_HEAD_DIM) & (lane < (hd + 1) * MEM_HEAD_DIM) for hd in range(MEM_HEADS)]


def _mem_attn_prompt_kernel(qm_ref, k_ref, v_ref, o_ref):
    qm = qm_ref[...]
    k = k_ref[...].astype(BF16)
    v = v_ref[...].astype(BF16)
    masks = _head_lane_mask(qm.shape[0])
    out = jnp.zeros(qm.shape, F32)
    for hd in range(MEM_HEADS):
        s = _dot_nt(jnp.where(masks[hd], qm, 0.0).astype(BF16), k)
        e = jnp.exp(s - jnp.max(s, axis=-1, keepdims=True))
        p = (e / jnp.sum(e, axis=-1, keepdims=True)).astype(BF16)
        out = out + jnp.where(masks[hd], _dot(p, v), 0.0)
    o_ref[...] = out.astype(o_ref.dtype)


def _mem_attn_prompt(qm, mem_k, mem_v, *, seq, tm):
    rows = qm.shape[0]
    tiles_per_seq = seq // tm
    kv_spec = pl.BlockSpec((None, N_MEM, MEM_WIDTH), lambda i: (i // tiles_per_seq, 0, 0))
    return pl.pallas_call(
        _mem_attn_prompt_kernel, grid=(rows // tm,),
        in_specs=[pl.BlockSpec((tm, MEM_WIDTH), lambda i: (i, 0)), kv_spec, kv_spec],
        out_specs=pl.BlockSpec((tm, MEM_WIDTH), lambda i: (i, 0)),
        out_shape=jax.ShapeDtypeStruct((rows, MEM_WIDTH), BF16),
        compiler_params=_params("arbitrary"), name="mem_attn_prompt",
    )(qm, mem_k, mem_v)


def _mem_attn_sample_kernel(qm_ref, kt_ref, vt_ref, o_ref, *, group):
    sub = lax.broadcasted_iota(jnp.int32, (8, MEM_WIDTH), 0)
    lane = lax.broadcasted_iota(jnp.int32, (8, MEM_WIDTH), 1)
    own = (lane >= sub * MEM_HEAD_DIM) & (lane < (sub + 1) * MEM_HEAD_DIM)
    for i in range(group):
        q_rows = jnp.where(own, qm_ref[i:i + 1, :], 0.0).astype(BF16)
        s = _dot(q_rows, kt_ref[i].astype(BF16))
        e = jnp.exp(s - jnp.max(s, axis=-1, keepdims=True))
        p = (e / jnp.sum(e, axis=-1, keepdims=True)).astype(BF16)
        o_all = _dot_nt(p, vt_ref[i].astype(BF16))
        o_ref[i:i + 1, :] = jnp.sum(jnp.where(own, o_all, 0.0), axis=0, keepdims=True).astype(o_ref.dtype)


def _mem_attn_sample(qm, cache_k, cache_v, *, layer, group):
    rows = qm.shape[0]
    kv_spec = pl.BlockSpec((None, group, MEM_WIDTH, N_MEM), lambda i: (layer, i, 0, 0))
    return pl.pallas_call(
        functools.partial(_mem_attn_sample_kernel, group=group), grid=(rows // group,),
        in_specs=[pl.BlockSpec((group, MEM_WIDTH), lambda i: (i, 0)), kv_spec, kv_spec],
        out_specs=pl.BlockSpec((group, MEM_WIDTH), lambda i: (i, 0)),
        out_shape=jax.ShapeDtypeStruct((rows, MEM_WIDTH), F32),
        compiler_params=_params("arbitrary"), name="mem_attn_sample",
    )(qm, cache_k, cache_v)


def _mlp_kernel(x1_ref, o_ref, wmo_ref, g3_ref, g4_ref, w1_ref, w2_ref, g5_ref, x3_ref, *, ff_chunk):
    x2 = x1_ref[...] + _rms(_dot(o_ref[...].astype(BF16), wmo_ref[...]), g3_ref[...])
    h = _rms(x2, g4_ref[...]).astype(BF16)
    f = None
    for c in range(D_FF // ff_chunk):
        cs = slice(c * ff_chunk, (c + 1) * ff_chunk)
        a = jnp.maximum(_dot(h, w1_ref[:, cs]), 0.0)
        part = _dot((a * a).astype(BF16), w2_ref[cs, :])
        f = part if f is None else f + part
    x3_ref[...] = x2 + _rms(f, g5_ref[...])


def _mlp(x1, o, lw, *, tm):
    rows = x1.shape[0]
    row_spec = lambda w: pl.BlockSpec((tm, w), lambda i: (i, 0))
    return pl.pallas_call(
        functools.partial(_mlp_kernel, ff_chunk=1024), grid=(rows // tm,),
        in_specs=[row_spec(D_MODEL), row_spec(MEM_WIDTH), _const_spec((MEM_WIDTH, D_MODEL)),
                  _const_spec((1, D_MODEL)), _const_spec((1, D_MODEL)), _const_spec((D_MODEL, D_FF)),
                  _const_spec((D_FF, D_MODEL)), _const_spec((1, D_MODEL))],
        out_specs=row_spec(D_MODEL),
        out_shape=jax.ShapeDtypeStruct((rows, D_MODEL), F32),
        compiler_params=_params("arbitrary"), name="mlp",
    )(x1, o, lw["w_mo"], lw["g3"], lw["g4"], lw["w_ff1"], lw["w_ff2"], lw["g5"])


def _rotate_half_cols(w):
    half = ROPE // 2
    return jnp.concatenate([-w[..., half:], w[..., :half]], axis=-1)


def _rope_tables(pos):
    half = ROPE // 2
    inv = 1.0 / (ROPE_THETA ** (jnp.arange(half, dtype=F32) / half))
    ang = pos.astype(F32)[:, None] * inv[None, :]
    cos, sin = jnp.cos(ang), jnp.sin(ang)
    n = pos.shape[0]
    cos_blk = jnp.concatenate([cos, cos, jnp.ones((n, HEAD_BLOCK - ROPE), F32)], axis=1)
    sin_blk = jnp.concatenate([sin, sin, jnp.zeros((n, HEAD_BLOCK - ROPE), F32)], axis=1)
    return cos_blk, sin_blk


def _layer_weights(l, norm_gains, w_in, a_v_norm, a_w_s, a_b_s, q_norm, w_uq, kv_norm, w_ukv, conv_w,
                   conv_b, conv_ln_g, conv_ln_b, w_branch, w_out, w_mq, w_mo, w_ff1, w_ff2):
    row = lambda v: v.reshape(1, -1)
    wi = w_in[l]
    o_b = 2 * BRANCH
    kr = wi[:, o_b + Q_LORA + KV_LORA:o_b + Q_LORA + KV_LORA + ROPE]
    pad = jnp.zeros((D_MODEL, HEAD_BLOCK - ROPE), F32)
    o_c = o_b + Q_LORA + KV_LORA + ROPE
    w_in_ext = jnp.concatenate([
        wi[:, :o_b + Q_LORA + KV_LORA], kr, pad, _rotate_half_cols(kr), pad, wi[:, o_c:]], axis=1)

    uq = w_uq[l].reshape(Q_LORA, MLA_HEADS, NOPE + ROPE)
    uq_nope, uq_rope = uq[..., :NOPE], uq[..., NOPE:]
    z = lambda n: jnp.zeros((Q_LORA, MLA_HEADS, n), F32)
    uq_pad = jnp.concatenate([uq_rope, uq_nope, z(HEAD_BLOCK - ROPE - NOPE)], axis=-1)
    uq_rot = jnp.concatenate([_rotate_half_cols(uq_rope), z(HEAD_BLOCK - ROPE)], axis=-1)
    w_uq_ext = jnp.concatenate([uq_pad.reshape(Q_LORA, QK_WIDTH), uq_rot.reshape(Q_LORA, QK_WIDTH)], axis=1)

    ukv = w_ukv[l].reshape(KV_LORA, MLA_HEADS, NOPE + V_DIM)
    uk, uv = ukv[..., :NOPE], ukv[..., NOPE:]
    zk = lambda n: jnp.zeros((KV_LORA, MLA_HEADS, n), F32)
    kn_pad = jnp.concatenate([zk(ROPE), uk, zk(HEAD_BLOCK - ROPE - NOPE)], axis=-1)
    w_k = kn_pad.reshape(KV_LORA, QK_WIDTH)
    w_vt = uv.reshape(KV_LORA, BRANCH).T
    uk_t = jnp.transpose(uk, (1, 2, 0))
    w_uk_abs = jnp.concatenate([jnp.zeros((MLA_HEADS, ROPE, KV_LORA), F32), uk_t,
                                jnp.zeros((MLA_HEADS, HEAD_BLOCK - ROPE - NOPE, KV_LORA), F32)], axis=1)
    eye = jnp.eye(MLA_HEADS, dtype=F32)
    w_uv_abs = (jnp.transpose(uv, (1, 0, 2))[:, :, None, :] * eye[:, None, :, None]).reshape(
        MLA_HEADS, KV_LORA, BRANCH)

    g = norm_gains[l]
    return {
        "g0": row(g[0]), "g1": row(g[1]), "g2": row(g[2]), "g3": row(g[3]), "g4": row(g[4]), "g5": row(g[5]),
        "w_in": w_in_ext.astype(BF16),
        "a_v_norm": row(a_v_norm[l]), "a_w_s": a_w_s[l], "a_b_s_t": a_b_s[l].T,
        "a_w_s0": row(jnp.repeat(a_w_s[l][:, 0, 0], CHUNK)), "a_b_s0": row(jnp.repeat(a_b_s[l][:, 0], CHUNK)),
        "q_norm": row(q_norm[l]), "kv_norm": row(kv_norm[l]),
        "w_uq": w_uq_ext.astype(BF16), "w_k": w_k.astype(BF16), "w_vt": w_vt.astype(BF16),
        "w_uk_abs": w_uk_abs.astype(BF16), "w_uv_abs": w_uv_abs.astype(BF16),
        "conv_w": conv_w[l], "conv_b": row(conv_b[l]), "conv_ln_g": row(conv_ln_g[l]),
        "conv_ln_b": row(conv_ln_b[l]),
        "w_branch": w_branch[l].astype(BF16), "w_out": w_out[l].astype(BF16),
        "w_mq": w_mq[l].astype(BF16), "w_mo": w_mo[l].astype(BF16),
        "w_ff1": w_ff1[l].astype(BF16), "w_ff2": w_ff2[l].astype(BF16),
    }


def _row_tile(rows, want):
    t = min(rows, want)
    while rows % t:
        t //= 2
    return t


def kernel(x_prompt, x_sample, mem_prompt, cache_latent, cache_conv, cache_mem_k, cache_mem_v, page_table,
           norm_gains, mem_norm, w_in, a_v_norm, a_w_s, a_b_s, q_norm, w_uq, kv_norm, w_ukv, conv_w, conv_b,
           conv_ln_g, conv_ln_b, w_branch, w_out, w_mq, w_mk, w_mv, w_mo, w_ff1, w_ff2):
    depth = w_in.shape[0]
    n_seq, seq, _ = x_prompt.shape
    n_s, t_s, _ = x_sample.shape
    assert t_s == 1 and seq % CHUNK == 0 and seq >= CONV_HALO
    n_pages = page_table.shape[1]
    past = n_pages * PAGE

    tm_mix = _row_tile(seq, 512)
    tm_row = _row_tile(seq, 512)
    t_attn = _row_tile(seq, 512)
    tm_s = n_s
    mem_group = _row_tile(n_s, 8)

    cos_p, sin_p = _rope_tables(jnp.arange(seq, dtype=jnp.int32))
    cos_s, sin_s = _rope_tables(jnp.full((n_s,), past, jnp.int32))

    xp = x_prompt.reshape(n_seq * seq, D_MODEL)
    xs = x_sample.reshape(n_s, D_MODEL)
    mem = mem_prompt.reshape(n_seq * N_MEM, D_MODEL)
    cache_k = jnp.transpose(cache_mem_k, (0, 1, 3, 4, 2)).reshape(depth, n_s, MEM_WIDTH, N_MEM)
    cache_v = jnp.transpose(cache_mem_v, (0, 1, 3, 4, 2)).reshape(depth, n_s, MEM_WIDTH, N_MEM)
    cache_lat = jnp.transpose(cache_latent, (0, 1, 3, 2))

    lat_p_l, lat_s_l, conv_p_l, conv_s_l, v_s_l, mk_p_l, mv_p_l = [], [], [], [], [], [], []
    for l in range(depth):
        lw = _layer_weights(l, norm_gains, w_in, a_v_norm, a_w_s, a_b_s, q_norm, w_uq, kv_norm, w_ukv,
                            conv_w, conv_b, conv_ln_g, conv_ln_b, w_branch, w_out, w_mq, w_mo, w_ff1, w_ff2)

        mk, mv = _memory_kv(mem, mem_norm[l].reshape(1, -1), w_mk[l].astype(BF16), w_mv[l].astype(BF16))
        out_a, q, lat, k, vt, out_c, gates, tail = _mix_prompt(xp, cos_p, sin_p, lw, tm=tm_mix)
        out_b = _attn_prompt(q, k, vt, seq=seq, t=t_attn)
        x1, qm = _merge(xp, out_a, out_b, out_c, gates, lw, tm=tm_row)
        o_mem = _mem_attn_prompt(qm, mk.reshape(n_seq, N_MEM, MEM_WIDTH), mv.reshape(n_seq, N_MEM, MEM_WIDTH),
                                 seq=seq, tm=tm_row)
        xp = _mlp(x1, o_mem, lw, tm=tm_row)
        lat_p_l.append(lat.reshape(n_seq, seq, LATENT))
        conv_p_l.append(tail[:, CONV_HALO - (CONV_W - 1):, :])
        mk_p_l.append(mk.reshape(n_seq, N_MEM, MEM_HEADS, MEM_HEAD_DIM))
        mv_p_l.append(mv.reshape(n_seq, N_MEM, MEM_HEADS, MEM_HEAD_DIM))

        conv_t = jnp.transpose(cache_conv[l], (1, 0, 2))
        out_a, q, lat, out_c, gates, v_n, glu = _mix_sample(xs, cos_s, sin_s, lw, conv_t)
        q_lat, q_rope = _q_latent(q, lw["w_uk_abs"])
        o_lat = _attn_sample(page_table, jnp.transpose(q_lat, (1, 0, 2)), jnp.transpose(q_rope, (1, 0, 2)),
                             lat.reshape(n_s, 1, LATENT), cache_lat, layer=l)
        out_b = _v_up(jnp.transpose(o_lat, (1, 0, 2)), lw["w_uv_abs"])
        x1, qm = _merge(xs, out_a, out_b, out_c, gates, lw, tm=tm_s)
        o_mem = _mem_attn_sample(qm, cache_k, cache_v, layer=l, group=mem_group)
        xs = _mlp(x1, o_mem, lw, tm=tm_s)
        lat_s_l.append(lat.reshape(n_s, 1, LATENT))
        conv_s_l.append(jnp.concatenate([cache_conv[l][:, 1:, :], glu[:, None, :]], axis=1))
        v_s_l.append(v_n.reshape(n_s, 1, BRANCH))

    return (xp.reshape(n_seq, seq, D_MODEL), xs.reshape(n_s, 1, D_MODEL),
            jnp.stack(lat_p_l), jnp.stack(lat_s_l), jnp.stack(conv_p_l), jnp.stack(conv_s_l),
            jnp.stack(v_s_l), jnp.stack(mk_p_l), jnp.stack(mv_p_l))
```

```python
import functools

import jax
import jax.numpy as jnp
import numpy as np
from jax import lax
from jax.experimental import pallas as pl
from jax.experimental.pallas import tpu as pltpu

F32 = jnp.float32
BF16 = jnp.bfloat16

D_MODEL = 1024
BRANCH = 512
N_BRANCH = 3
CHUNK = 128
A_GROUPS = 4
MLA_HEADS = 8
Q_LORA = 256
KV_LORA = 256
NOPE = 64
ROPE = 32
V_DIM = 64
LATENT = KV_LORA + ROPE
ROPE_THETA = 10000.0
CONV_W = 31
N_MEM = 256
MEM_HEADS = 4
MEM_HEAD_DIM = 64
MEM_WIDTH = MEM_HEADS * MEM_HEAD_DIM
D_FF = 4 * D_MODEL
EPS = 1e-6
PAGE = 128

HEAD_BLOCK = 128
V_AUG = V_DIM + 16
QK_WIDTH = MLA_HEADS * HEAD_BLOCK
Q_SCALE = (NOPE + ROPE) ** -0.5
MEM_SCALE = MEM_HEAD_DIM ** -0.5
LOG2_E = float(np.log2(np.e))

COL_A = 0
COL_B = COL_A + 2 * BRANCH
COL_C = COL_B + Q_LORA + KV_LORA + 2 * HEAD_BLOCK
COL_G = COL_C + 2 * BRANCH
IN_EXT = COL_G + N_BRANCH * D_MODEL

CONV_HALO = 32
VMEM_LIMIT = 56 * 1024 * 1024


def _params(*sem):
    return pltpu.CompilerParams(dimension_semantics=sem, vmem_limit_bytes=VMEM_LIMIT)


def _const_spec(shape):
    n = len(shape)
    return pl.BlockSpec(shape, lambda *_: (0,) * n, pipeline_mode=pl.Buffered(1))


def _dot(a, b):
    return jnp.dot(a, b, preferred_element_type=F32)


def _dot_nt(a, b):
    return lax.dot_general(a, b, (((1,), (1,)), ((), ())), preferred_element_type=F32)


def _rms(x, g):
    return x * lax.rsqrt(jnp.mean(x * x, axis=-1, keepdims=True) + EPS) * g


def _gelu(x):
    c = np.sqrt(2.0 / np.pi).astype(np.float32)
    return x * (0.5 * (1.0 + jnp.tanh(c * (x + 0.044715 * (x * x * x)))))


def _sigmoid(x):
    return 1.0 / (1.0 + jnp.exp(-x))


def _silu_layer_norm(x, g, b):
    mu = jnp.mean(x, axis=-1, keepdims=True)
    xc = x - mu
    var = jnp.mean(xc * xc, axis=-1, keepdims=True)
    y = xc * lax.rsqrt(var + EPS) * g + b
    return y * _sigmoid(y)


def _mix_front(x_ref, g0_ref, w_in_ref, cos_ref, sin_ref, avn_ref, qn_ref, kvn_ref, w_uq_ref,
               q_ref, lat_ref, gates_ref, q_scale):
    h = _rms(x_ref[...], g0_ref[...]).astype(BF16)

    za = _dot(h, w_in_ref[:, COL_A:COL_B])
    u = _gelu(za[:, :BRANCH])
    v_n = _rms(_gelu(za[:, BRANCH:]), avn_ref[...])

    zb = _dot(h, w_in_ref[:, COL_B:COL_C])
    c_q = _rms(zb[:, :Q_LORA], qn_ref[...])
    c_kv = _rms(zb[:, Q_LORA:Q_LORA + KV_LORA], kvn_ref[...])
    cos = cos_ref[...]
    sin = sin_ref[...]
    o = Q_LORA + KV_LORA
    k_rope = zb[:, o:o + HEAD_BLOCK] * cos + zb[:, o + HEAD_BLOCK:o + 2 * HEAD_BLOCK] * sin

    qq = _dot(c_q.astype(BF16), w_uq_ref[...]) * q_scale
    for hd in range(MLA_HEADS):
        lo = hd * HEAD_BLOCK
        q_h = qq[:, lo:lo + HEAD_BLOCK] * cos + qq[:, QK_WIDTH + lo:QK_WIDTH + lo + HEAD_BLOCK] * sin
        q_ref[:, lo:lo + HEAD_BLOCK] = q_h.astype(q_ref.dtype)

    lat_ref[:, :KV_LORA] = c_kv
    lat_ref[:, KV_LORA:] = k_rope[:, :ROPE]

    zc = _dot(h, w_in_ref[:, COL_C:COL_G])
    glu = zc[:, :BRANCH] * _sigmoid(zc[:, BRANCH:])

    for k in range(N_BRANCH):
        lo = COL_G + k * D_MODEL
        gates_ref[:, k * D_MODEL:(k + 1) * D_MODEL] = _sigmoid(_dot(h, w_in_ref[:, lo:lo + D_MODEL]))
    return u, v_n, c_kv, k_rope, glu


def _mix_prompt_kernel(x_ref, cos_ref, sin_ref, g0_ref, w_in_ref, avn_ref, ws_ref, bs_ref, qn_ref,
                       kvn_ref, w_uq_ref, w_k_ref, w_vt_ref, cw_ref, cb_ref, lng_ref, lnb_ref,
                       outa_ref, q_ref, lat_ref, k_ref, vt_ref, outc_ref, gates_ref, tail_ref,
                       xp_ref, sh_ref, *, tm, tiles_per_seq):
    u, v_n, c_kv, k_rope, glu = _mix_front(x_ref, g0_ref, w_in_ref, cos_ref, sin_ref, avn_ref,
                                           qn_ref, kvn_ref, w_uq_ref, q_ref, lat_ref, gates_ref,
                                           Q_SCALE * LOG2_E)

    row = lax.broadcasted_iota(jnp.int32, (CHUNK, CHUNK), 0)
    col = lax.broadcasted_iota(jnp.int32, (CHUNK, CHUNK), 1)
    causal = col <= row
    for g in range(A_GROUPS):
        w_g = jnp.where(causal, ws_ref[g], 0.0).astype(BF16)
        b_g = bs_ref[:, g:g + 1]
        for c in range(tm // CHUNK):
            rs = slice(c * CHUNK, (c + 1) * CHUNK)
            cs = slice(g * CHUNK, (g + 1) * CHUNK)
            mixed = _dot(w_g, v_n[rs, cs].astype(BF16)) + b_g
            outa_ref[rs, cs] = (u[rs, cs] * mixed).astype(outa_ref.dtype)

    c_kv16 = c_kv.astype(BF16)
    k_nope = _dot(c_kv16, w_k_ref[...])
    for hd in range(MLA_HEADS):
        lo = hd * HEAD_BLOCK
        k_ref[:, lo:lo + HEAD_BLOCK] = (k_nope[:, lo:lo + HEAD_BLOCK] + k_rope).astype(k_ref.dtype)
    v_t = _dot_nt(w_vt_ref[...], c_kv16).astype(vt_ref.dtype)
    ones_rows = jnp.ones((V_AUG - V_DIM, tm), vt_ref.dtype)
    for hd in range(MLA_HEADS):
        vt_ref[hd * V_AUG:hd * V_AUG + V_DIM, :] = v_t[hd * V_DIM:(hd + 1) * V_DIM, :]
        vt_ref[hd * V_AUG + V_DIM:(hd + 1) * V_AUG, :] = ones_rows

    first = (pl.program_id(0) % tiles_per_seq) == 0

    @pl.when(first)
    def _():
        xp_ref[:CONV_HALO, :] = jnp.zeros((CONV_HALO, BRANCH), F32)

    @pl.when(jnp.logical_not(first))
    def _():
        xp_ref[:CONV_HALO, :] = xp_ref[tm:tm + CONV_HALO, :]

    xp_ref[CONV_HALO:, :] = glu
    acc = jnp.zeros((tm, BRANCH), F32) + cb_ref[...]
    off = CONV_HALO - (CONV_W - 1)
    for phase in range(8):
        taps = [j for j in range(CONV_W) if (off + j) % 8 == phase]
        if not taps:
            continue
        base = min(off + j for j in taps)
        span = max(off + j for j in taps) - base + tm
        if phase == 0:
            src_ref, src_base = xp_ref, base
        else:
            sh_ref[:span, :] = xp_ref[base:base + span, :]
            src_ref, src_base = sh_ref, 0
        for j in taps:
            lo = src_base + off + j - base
            acc = acc + cw_ref[j:j + 1, :] * src_ref[lo:lo + tm, :]
    outc_ref[...] = _silu_layer_norm(acc, lng_ref[...], lnb_ref[...]).astype(outc_ref.dtype)
    tail_ref[...] = xp_ref[tm:tm + CONV_HALO, :]


def _mix_sample_kernel(x_ref, cos_ref, sin_ref, g0_ref, w_in_ref, avn_ref, ws0_ref, bs0_ref, qn_ref,
                       kvn_ref, w_uq_ref, cw_ref, cb_ref, lng_ref, lnb_ref, conv_ref,
                       outa_ref, q_ref, lat_ref, outc_ref, gates_ref, vn_ref, glu_ref):
    u, v_n, _, _, glu = _mix_front(x_ref, g0_ref, w_in_ref, cos_ref, sin_ref, avn_ref,
                                   qn_ref, kvn_ref, w_uq_ref, q_ref, lat_ref, gates_ref, Q_SCALE)
    vn_ref[...] = v_n
    glu_ref[...] = glu
    outa_ref[...] = (u * (ws0_ref[...] * v_n + bs0_ref[...])).astype(outa_ref.dtype)
    acc = cb_ref[...] + cw_ref[CONV_W - 1:CONV_W, :] * glu
    for j in range(CONV_W - 1):
        acc = acc + cw_ref[j:j + 1, :] * conv_ref[j]
    outc_ref[...] = _silu_layer_norm(acc, lng_ref[...], lnb_ref[...]).astype(outc_ref.dtype)


def _mix_prompt(x, cos, sin, lw, *, tm):
    rows = x.shape[0]
    seq = cos.shape[0]
    tiles_per_seq = seq // tm
    n_seq = rows // seq
    grid = (rows // tm,)
    row_spec = lambda w: pl.BlockSpec((tm, w), lambda i: (i, 0))
    pos_spec = pl.BlockSpec((tm, HEAD_BLOCK), lambda i: (i % tiles_per_seq, 0))
    in_specs = [
        row_spec(D_MODEL), pos_spec, pos_spec,
        _const_spec((1, D_MODEL)), _const_spec((D_MODEL, IN_EXT)), _const_spec((1, BRANCH)),
        _const_spec((A_GROUPS, CHUNK, CHUNK)), _const_spec((CHUNK, A_GROUPS)),
        _const_spec((1, Q_LORA)), _const_spec((1, KV_LORA)),
        _const_spec((Q_LORA, 2 * QK_WIDTH)), _const_spec((KV_LORA, QK_WIDTH)),
        _const_spec((BRANCH, KV_LORA)),
        _const_spec((CONV_W, BRANCH)), _const_spec((1, BRANCH)), _const_spec((1, BRANCH)),
        _const_spec((1, BRANCH)),
    ]
    out_shape = (
        jax.ShapeDtypeStruct((rows, BRANCH), BF16),
        jax.ShapeDtypeStruct((rows, QK_WIDTH), BF16),
        jax.ShapeDtypeStruct((rows, LATENT), F32),
        jax.ShapeDtypeStruct((rows, QK_WIDTH), BF16),
        jax.ShapeDtypeStruct((n_seq, tiles_per_seq, MLA_HEADS * V_AUG, tm), BF16),
        jax.ShapeDtypeStruct((rows, BRANCH), BF16),
        jax.ShapeDtypeStruct((rows, N_BRANCH * D_MODEL), F32),
        jax.ShapeDtypeStruct((n_seq, CONV_HALO, BRANCH), F32),
    )
    out_specs = (
        row_spec(BRANCH), row_spec(QK_WIDTH), row_spec(LATENT), row_spec(QK_WIDTH),
        pl.BlockSpec((None, None, MLA_HEADS * V_AUG, tm),
                     lambda i: (i // tiles_per_seq, i % tiles_per_seq, 0, 0)),
        row_spec(BRANCH), row_spec(N_BRANCH * D_MODEL),
        pl.BlockSpec((None, CONV_HALO, BRANCH), lambda i: (i // tiles_per_seq, 0, 0)),
    )
    return pl.pallas_call(
        functools.partial(_mix_prompt_kernel, tm=tm, tiles_per_seq=tiles_per_seq),
        grid=grid, in_specs=in_specs, out_specs=out_specs, out_shape=out_shape,
        scratch_shapes=[pltpu.VMEM((tm + CONV_HALO, BRANCH), F32)] * 2,
        compiler_params=_params("arbitrary"), name="mix_prompt",
    )(x, cos, sin, lw["g0"], lw["w_in"], lw["a_v_norm"], lw["a_w_s"], lw["a_b_s_t"], lw["q_norm"],
      lw["kv_norm"], lw["w_uq"], lw["w_k"], lw["w_vt"], lw["conv_w"], lw["conv_b"], lw["conv_ln_g"],
      lw["conv_ln_b"])


def _mix_sample(x, cos, sin, lw, conv_t):
    rows = x.shape[0]
    full = lambda shape: pl.BlockSpec(shape, lambda i: (0,) * len(shape))
    in_specs = [
        full((rows, D_MODEL)), full((rows, HEAD_BLOCK)), full((rows, HEAD_BLOCK)),
        full((1, D_MODEL)), _const_spec((D_MODEL, IN_EXT)), full((1, BRANCH)),
        full((1, BRANCH)), full((1, BRANCH)), full((1, Q_LORA)), full((1, KV_LORA)),
        full((Q_LORA, 2 * QK_WIDTH)), full((CONV_W, BRANCH)), full((1, BRANCH)), full((1, BRANCH)),
        full((1, BRANCH)), full((CONV_W - 1, rows, BRANCH)),
    ]
    out_shape = (
        jax.ShapeDtypeStruct((rows, BRANCH), BF16),
        jax.ShapeDtypeStruct((rows, QK_WIDTH), F32),
        jax.ShapeDtypeStruct((rows, LATENT), F32),
        jax.ShapeDtypeStruct((rows, BRANCH), BF16),
        jax.ShapeDtypeStruct((rows, N_BRANCH * D_MODEL), F32),
        jax.ShapeDtypeStruct((rows, BRANCH), F32),
        jax.ShapeDtypeStruct((rows, BRANCH), F32),
    )
    out_specs = tuple(full(s.shape) for s in out_shape)
    return pl.pallas_call(
        _mix_sample_kernel, grid=(1,), in_specs=in_specs, out_specs=out_specs, out_shape=out_shape,
        compiler_params=_params("arbitrary"), name="mix_sample",
    )(x, cos, sin, lw["g0"], lw["w_in"], lw["a_v_norm"], lw["a_w_s0"], lw["a_b_s0"], lw["q_norm"],
      lw["kv_norm"], lw["w_uq"], lw["conv_w"], lw["conv_b"], lw["conv_ln_g"], lw["conv_ln_b"], conv_t)


def _attn_prompt_kernel(qi_ref, ki_ref, q_ref, k_ref, vt_ref, o_ref, m_ref, acc_ref, *, tq, tk):
    step = pl.program_id(1)
    qi = qi_ref[step]
    ki = ki_ref[step]
    ratio = tq // tk

    @pl.when(ki == 0)
    def _():
        m_ref[...] = jnp.full(m_ref.shape, -jnp.inf, F32)
        acc_ref[...] = jnp.zeros(acc_ref.shape, F32)

    def update(masked):
        if masked:
            key = lax.broadcasted_iota(jnp.int32, (tk, tq), 0)
            qry = lax.broadcasted_iota(jnp.int32, (tk, tq), 1)
            keep = key <= qry if ratio == 1 else key - qry <= qi * tq - ki * tk

        def scores(hd):
            lo = hd * HEAD_BLOCK
            return _dot_nt(k_ref[:, lo:lo + HEAD_BLOCK], q_ref[:, lo:lo + HEAD_BLOCK])

        ahead = 2
        pending = [scores(hd) for hd in range(ahead)]
        for hd in range(MLA_HEADS):
            vs = slice(hd * V_DIM, (hd + 1) * V_DIM)
            s = pending.pop(0)
            if hd + ahead < MLA_HEADS:
                pending.append(scores(hd + ahead))
            if masked:
                s = jnp.where(keep, s, -jnp.inf)
            m_old = m_ref[hd]
            m_new = jnp.maximum(m_old, jnp.max(s, axis=0, keepdims=True))
            alpha = jnp.exp2(m_old - m_new)
            p = jnp.exp2(s - m_new)
            m_ref[hd] = m_new
            v_aug = vt_ref[hd * V_AUG:(hd + 1) * V_AUG, :]
            acc_ref[hd] = acc_ref[hd] * alpha + _dot(v_aug, p.astype(BF16))

    @pl.when(ki < qi * ratio)
    def _():
        update(False)

    @pl.when(ki >= qi * ratio)
    def _():
        update(True)

    @pl.when(ki == qi * ratio + ratio - 1)
    def _():
        outs = []
        for hd in range(MLA_HEADS):
            acc = acc_ref[hd]
            outs.append(acc[:V_DIM, :] / acc[V_DIM:V_DIM + 1, :])
        o_ref[...] = jnp.concatenate(outs, axis=0).T.astype(o_ref.dtype)


def _attn_prompt(q, k, vt, *, seq, tq, tk):
    rows = q.shape[0]
    n_seq = rows // seq
    nq, nk, ratio = seq // tq, seq // tk, tq // tk
    assert vt.shape == (n_seq, nk, MLA_HEADS * V_AUG, tk), vt.shape
    pairs = [(i, j) for i in range(nq) for j in range((i + 1) * ratio)]
    qi_tab = np.array([i for i, _ in pairs], np.int32)
    ki_tab = np.array([j for _, j in pairs], np.int32)
    grid_spec = pltpu.PrefetchScalarGridSpec(
        num_scalar_prefetch=2,
        grid=(n_seq, len(pairs)),
        in_specs=[
            pl.BlockSpec((tq, QK_WIDTH), lambda b, s, qi, ki: (b * nq + qi[s], 0)),
            pl.BlockSpec((tk, QK_WIDTH), lambda b, s, qi, ki: (b * nk + ki[s], 0)),
            pl.BlockSpec((None, None, MLA_HEADS * V_AUG, tk), lambda b, s, qi, ki: (b, ki[s], 0, 0)),
        ],
        out_specs=pl.BlockSpec((tq, BRANCH), lambda b, s, qi, ki: (b * nq + qi[s], 0)),
        scratch_shapes=[
            pltpu.VMEM((MLA_HEADS, 1, tq), F32),
            pltpu.VMEM((MLA_HEADS, V_AUG, tq), F32),
        ],
    )
    return pl.pallas_call(
        functools.partial(_attn_prompt_kernel, tq=tq, tk=tk),
        grid_spec=grid_spec, out_shape=jax.ShapeDtypeStruct((rows, BRANCH), BF16),
        compiler_params=_params("arbitrary", "arbitrary"), name="attn_prompt",
    )(jnp.asarray(qi_tab), jnp.asarray(ki_tab), q, k, vt)


def _q_latent_kernel(q_ref, w_uk_ref, qlat_ref, qrope_ref):
    lane = lax.broadcasted_iota(jnp.int32, (q_ref.shape[0], HEAD_BLOCK), 1)
    for hd in range(MLA_HEADS):
        q_h = q_ref[:, hd * HEAD_BLOCK:(hd + 1) * HEAD_BLOCK]
        qlat_ref[hd] = _dot(q_h.astype(BF16), w_uk_ref[hd])
        qrope_ref[hd] = jnp.where(lane < ROPE, q_h, 0.0)


def _q_latent(q, w_uk):
    rows = q.shape[0]
    full = lambda shape: pl.BlockSpec(shape, lambda i: (0,) * len(shape))
    return pl.pallas_call(
        _q_latent_kernel, grid=(1,),
        in_specs=[full((rows, QK_WIDTH)), full((MLA_HEADS, HEAD_BLOCK, KV_LORA))],
        out_specs=(full((MLA_HEADS, rows, KV_LORA)), full((MLA_HEADS, rows, HEAD_BLOCK))),
        out_shape=(jax.ShapeDtypeStruct((MLA_HEADS, rows, KV_LORA), F32),
                   jax.ShapeDtypeStruct((MLA_HEADS, rows, HEAD_BLOCK), F32)),
        compiler_params=_params("arbitrary"), name="q_latent",
    )(q, w_uk)


def _attn_sample_kernel(pt_ref, qlat_ref, qrope_ref, latn_ref, cache_ref, o_ref, buf_ref, sem_ref, *,
                        layer, n_pages, key_chunk):
    b = pl.program_id(0)
    nb = pl.num_programs(0)
    past = n_pages * PAGE

    def page_copy(sample, page, slot):
        return pltpu.make_async_copy(
            cache_ref.at[layer, pt_ref[sample, page]],
            buf_ref.at[slot, :, pl.ds(pl.multiple_of(page * PAGE, PAGE), PAGE)],
            sem_ref.at[slot])

    def start_all(sample, slot):
        def body(page, carry):
            page_copy(sample, page, slot).start()
            return carry
        lax.fori_loop(0, n_pages, body, 0)

    def wait_all(sample, slot):
        def body(page, carry):
            page_copy(sample, page, slot).wait()
            return carry
        lax.fori_loop(0, n_pages, body, 0)

    slot = b % 2

    @pl.when(b == 0)
    def _():
        start_all(0, 0)

    @pl.when(b + 1 < nb)
    def _():
        start_all(b + 1, 1 - slot)

    wait_all(b, slot)

    q_lat = qlat_ref[...].astype(BF16)
    q_rope = qrope_ref[:, :ROPE].astype(BF16)
    lat_new = latn_ref[...].astype(BF16).astype(F32)
    s_new = (jnp.sum(q_lat.astype(F32) * lat_new[:, :KV_LORA], axis=-1, keepdims=True)
             + jnp.sum(q_rope.astype(F32) * lat_new[:, KV_LORA:], axis=-1, keepdims=True))

    n_chunks = past // key_chunk
    scores = []
    for c in range(n_chunks):
        cols = buf_ref[slot, :, c * key_chunk:(c + 1) * key_chunk].astype(BF16)
        scores.append(_dot(q_lat, cols[:KV_LORA, :]) + _dot(q_rope, cols[KV_LORA:, :]))
    m = s_new
    for s in scores:
        m = jnp.maximum(m, jnp.max(s, axis=-1, keepdims=True))
    exps = [jnp.exp(s - m) for s in scores]
    e_new = jnp.exp(s_new - m)
    denom = e_new
    for e in exps:
        denom = denom + jnp.sum(e, axis=-1, keepdims=True)
    p_new = (e_new / denom).astype(BF16).astype(F32)
    out = p_new * lat_new[:, :KV_LORA]
    for c in range(n_chunks):
        p = (exps[c] / denom).astype(BF16)
        cols = buf_ref[slot, :KV_LORA, c * key_chunk:(c + 1) * key_chunk].astype(BF16)
        out = out + _dot_nt(p, cols)
    o_ref[...] = out


def _attn_sample(page_table, q_lat, q_rope, lat_new, cache_latent, *, layer):
    n_s, n_pages = page_table.shape
    past = n_pages * PAGE
    key_chunk = 1024 if past % 1024 == 0 else PAGE
    grid_spec = pltpu.PrefetchScalarGridSpec(
        num_scalar_prefetch=1,
        grid=(n_s,),
        in_specs=[
            pl.BlockSpec((None, MLA_HEADS, KV_LORA), lambda b, pt: (b, 0, 0)),
            pl.BlockSpec((None, MLA_HEADS, HEAD_BLOCK), lambda b, pt: (b, 0, 0)),
            pl.BlockSpec((None, 1, LATENT), lambda b, pt: (b, 0, 0)),
            pl.BlockSpec(memory_space=pl.ANY),
        ],
        out_specs=pl.BlockSpec((None, MLA_HEADS, KV_LORA), lambda b, pt: (b, 0, 0)),
        scratch_shapes=[
            pltpu.VMEM((2, LATENT, past), F32),
            pltpu.SemaphoreType.DMA((2,)),
        ],
    )
    return pl.pallas_call(
        functools.partial(_attn_sample_kernel, layer=layer, n_pages=n_pages, key_chunk=key_chunk),
        grid_spec=grid_spec, out_shape=jax.ShapeDtypeStruct((n_s, MLA_HEADS, KV_LORA), F32),
        compiler_params=_params("arbitrary"), name="attn_sample",
    )(page_table, q_lat, q_rope, lat_new, cache_latent)


def _v_up_kernel(o_ref, w_uv_ref, out_ref):
    acc = _dot(o_ref[0].astype(BF16), w_uv_ref[0])
    for hd in range(1, MLA_HEADS):
        acc = acc + _dot(o_ref[hd].astype(BF16), w_uv_ref[hd])
    out_ref[...] = acc.astype(out_ref.dtype)


def _v_up(o_lat, w_uv):
    rows = o_lat.shape[1]
    full = lambda shape: pl.BlockSpec(shape, lambda i: (0,) * len(shape))
    return pl.pallas_call(
        _v_up_kernel, grid=(1,),
        in_specs=[full((MLA_HEADS, rows, KV_LORA)), full((MLA_HEADS, KV_LORA, BRANCH))],
        out_specs=full((rows, BRANCH)),
        out_shape=jax.ShapeDtypeStruct((rows, BRANCH), BF16),
        compiler_params=_params("arbitrary"), name="v_up",
    )(o_lat, w_uv)


def _merge_kernel(x_ref, a_ref, b_ref, c_ref, gates_ref, wb_ref, wo_ref, g1_ref, g2_ref, wmq_ref,
                  x1_ref, qm_ref):
    merged = None
    for k, br_ref in enumerate((a_ref, b_ref, c_ref)):
        term = gates_ref[:, k * D_MODEL:(k + 1) * D_MODEL] * _dot(br_ref[...], wb_ref[k])
        merged = term if merged is None else merged + term
    y = _dot(merged.astype(BF16), wo_ref[...])
    x1 = x_ref[...] + _rms(y, g1_ref[...])
    x1_ref[...] = x1
    h = _rms(x1, g2_ref[...]).astype(BF16)
    qm_ref[...] = _dot(h, wmq_ref[...]) * MEM_SCALE


def _merge(x, out_a, out_b, out_c, gates, lw, *, tm):
    rows = x.shape[0]
    row_spec = lambda w: pl.BlockSpec((tm, w), lambda i: (i, 0))
    return pl.pallas_call(
        _merge_kernel, grid=(rows // tm,),
        in_specs=[row_spec(D_MODEL), row_spec(BRANCH), row_spec(BRANCH), row_spec(BRANCH),
                  row_spec(N_BRANCH * D_MODEL), _const_spec((N_BRANCH, BRANCH, D_MODEL)),
                  _const_spec((D_MODEL, D_MODEL)), _const_spec((1, D_MODEL)), _const_spec((1, D_MODEL)),
                  _const_spec((D_MODEL, MEM_WIDTH))],
        out_specs=(row_spec(D_MODEL), row_spec(MEM_WIDTH)),
        out_shape=(jax.ShapeDtypeStruct((rows, D_MODEL), F32),
                   jax.ShapeDtypeStruct((rows, MEM_WIDTH), F32)),
        compiler_params=_params("arbitrary"), name="merge",
    )(x, out_a, out_b, out_c, gates, lw["w_branch"], lw["w_out"], lw["g1"], lw["g2"], lw["w_mq"])


def _memory_kv_kernel(mem_ref, g_ref, wk_ref, wv_ref, k_ref, v_ref):
    m = _rms(mem_ref[...], g_ref[...]).astype(BF16)
    k_ref[...] = _dot(m, wk_ref[...])
    v_ref[...] = _dot(m, wv_ref[...])


def _memory_kv(mem, g, w_mk, w_mv):
    rows = mem.shape[0]
    full = lambda shape: pl.BlockSpec(shape, lambda i: (0,) * len(shape))
    return pl.pallas_call(
        _memory_kv_kernel, grid=(1,),
        in_specs=[full((rows, D_MODEL)), full((1, D_MODEL)), full((D_MODEL, MEM_WIDTH)),
                  full((D_MODEL, MEM_WIDTH))],
        out_specs=(full((rows, MEM_WIDTH)), full((rows, MEM_WIDTH))),
        out_shape=(jax.ShapeDtypeStruct((rows, MEM_WIDTH), F32),) * 2,
        compiler_params=_params("arbitrary"), name="memory_kv",
    )(mem, g, w_mk, w_mv)


def _head_lane_mask(rows):
    lane = lax.broadcasted_iota(jnp.int32, (rows, MEM_WIDTH), 1)
    return [(lane >= hd * MEM_HEAD_DIM) & (lane < (hd + 1) * MEM_HEAD_DIM) for hd in range(MEM_HEADS)]


def _mem_attn_prompt_kernel(qm_ref, k_ref, v_ref, o_ref):
    qm = qm_ref[...]
    k = k_ref[...].astype(BF16)
    v = v_ref[...].astype(BF16)
    masks = _head_lane_mask(qm.shape[0])
    out = jnp.zeros(qm.shape, F32)
    for hd in range(MEM_HEADS):
        s = _dot_nt(jnp.where(masks[hd], qm, 0.0).astype(BF16), k)
        e = jnp.exp(s - jnp.max(s, axis=-1, keepdims=True))
        p = (e / jnp.sum(e, axis=-1, keepdims=True)).astype(BF16)
        out = out + jnp.where(masks[hd], _dot(p, v), 0.0)
    o_ref[...] = out.astype(o_ref.dtype)


def _mem_attn_prompt(qm, mem_k, mem_v, *, seq, tm):
    rows = qm.shape[0]
    tiles_per_seq = seq // tm
    kv_spec = pl.BlockSpec((None, N_MEM, MEM_WIDTH), lambda i: (i // tiles_per_seq, 0, 0))
    return pl.pallas_call(
        _mem_attn_prompt_kernel, grid=(rows // tm,),
        in_specs=[pl.BlockSpec((tm, MEM_WIDTH), lambda i: (i, 0)), kv_spec, kv_spec],
        out_specs=pl.BlockSpec((tm, MEM_WIDTH), lambda i: (i, 0)),
        out_shape=jax.ShapeDtypeStruct((rows, MEM_WIDTH), BF16),
        compiler_params=_params("arbitrary"), name="mem_attn_prompt",
    )(qm, mem_k, mem_v)


def _mem_attn_sample_kernel(qm_ref, kt_ref, vt_ref, o_ref, *, group):
    sub = lax.broadcasted_iota(jnp.int32, (8, MEM_WIDTH), 0)
    lane = lax.broadcasted_iota(jnp.int32, (8, MEM_WIDTH), 1)
    own = (lane >= sub * MEM_HEAD_DIM) & (lane < (sub + 1) * MEM_HEAD_DIM)
    for i in range(group):
        q_rows = jnp.where(own, qm_ref[i:i + 1, :], 0.0).astype(BF16)
        s = _dot(q_rows, kt_ref[i].astype(BF16))
        e = jnp.exp(s - jnp.max(s, axis=-1, keepdims=True))
        p = (e / jnp.sum(e, axis=-1, keepdims=True)).astype(BF16)
        o_all = _dot_nt(p, vt_ref[i].astype(BF16))
        o_ref[i:i + 1, :] = jnp.sum(jnp.where(own, o_all, 0.0), axis=0, keepdims=True).astype(o_ref.dtype)


def _mem_attn_sample(qm, cache_k, cache_v, *, layer, group):
    rows = qm.shape[0]
    kv_spec = pl.BlockSpec((None, group, MEM_WIDTH, N_MEM), lambda i: (layer, i, 0, 0))
    return pl.pallas_call(
        functools.partial(_mem_attn_sample_kernel, group=group), grid=(rows // group,),
        in_specs=[pl.BlockSpec((group, MEM_WIDTH), lambda i: (i, 0)), kv_spec, kv_spec],
        out_specs=pl.BlockSpec((group, MEM_WIDTH), lambda i: (i, 0)),
        out_shape=jax.ShapeDtypeStruct((rows, MEM_WIDTH), F32),
        compiler_params=_params("arbitrary"), name="mem_attn_sample",
    )(qm, cache_k, cache_v)


def _mlp_kernel(x1_ref, o_ref, wmo_ref, g3_ref, g4_ref, w1_ref, w2_ref, g5_ref, x3_ref, *, ff_chunk):
    x2 = x1_ref[...] + _rms(_dot(o_ref[...].astype(BF16), wmo_ref[...]), g3_ref[...])
    h = _rms(x2, g4_ref[...]).astype(BF16)
    f = None
    for c in range(D_FF // ff_chunk):
        cs = slice(c * ff_chunk, (c + 1) * ff_chunk)
        a = jnp.maximum(_dot(h, w1_ref[:, cs]), 0.0)
        part = _dot((a * a).astype(BF16), w2_ref[cs, :])
        f = part if f is None else f + part
    x3_ref[...] = x2 + _rms(f, g5_ref[...])


def _mlp(x1, o, lw, *, tm):
    rows = x1.shape[0]
    row_spec = lambda w: pl.BlockSpec((tm, w), lambda i: (i, 0))
    return pl.pallas_call(
        functools.partial(_mlp_kernel, ff_chunk=1024), grid=(rows // tm,),
        in_specs=[row_spec(D_MODEL), row_spec(MEM_WIDTH), _const_spec((MEM_WIDTH, D_MODEL)),
                  _const_spec((1, D_MODEL)), _const_spec((1, D_MODEL)), _const_spec((D_MODEL, D_FF)),
                  _const_spec((D_FF, D_MODEL)), _const_spec((1, D_MODEL))],
        out_specs=row_spec(D_MODEL),
        out_shape=jax.ShapeDtypeStruct((rows, D_MODEL), F32),
        compiler_params=_params("arbitrary"), name="mlp",
    )(x1, o, lw["w_mo"], lw["g3"], lw["g4"], lw["w_ff1"], lw["w_ff2"], lw["g5"])


def _rotate_half_cols(w):
    half = ROPE // 2
    return jnp.concatenate([-w[..., half:], w[..., :half]], axis=-1)


def _rope_tables(pos):
    half = ROPE // 2
    inv = 1.0 / (ROPE_THETA ** (jnp.arange(half, dtype=F32) / half))
    ang = pos.astype(F32)[:, None] * inv[None, :]
    cos, sin = jnp.cos(ang), jnp.sin(ang)
    n = pos.shape[0]
    cos_blk = jnp.concatenate([cos, cos, jnp.ones((n, HEAD_BLOCK - ROPE), F32)], axis=1)
    sin_blk = jnp.concatenate([sin, sin, jnp.zeros((n, HEAD_BLOCK - ROPE), F32)], axis=1)
    return cos_blk, sin_blk


def _layer_weights(l, norm_gains, w_in, a_v_norm, a_w_s, a_b_s, q_norm, w_uq, kv_norm, w_ukv, conv_w,
                   conv_b, conv_ln_g, conv_ln_b, w_branch, w_out, w_mq, w_mo, w_ff1, w_ff2):
    row = lambda v: v.reshape(1, -1)
    wi = w_in[l]
    o_b = 2 * BRANCH
    kr = wi[:, o_b + Q_LORA + KV_LORA:o_b + Q_LORA + KV_LORA + ROPE]
    pad = jnp.zeros((D_MODEL, HEAD_BLOCK - ROPE), F32)
    o_c = o_b + Q_LORA + KV_LORA + ROPE
    w_in_ext = jnp.concatenate([
        wi[:, :o_b + Q_LORA + KV_LORA], kr, pad, _rotate_half_cols(kr), pad, wi[:, o_c:]], axis=1)

    uq = w_uq[l].reshape(Q_LORA, MLA_HEADS, NOPE + ROPE)
    uq_nope, uq_rope = uq[..., :NOPE], uq[..., NOPE:]
    z = lambda n: jnp.zeros((Q_LORA, MLA_HEADS, n), F32)
    uq_pad = jnp.concatenate([uq_rope, uq_nope, z(HEAD_BLOCK - ROPE - NOPE)], axis=-1)
    uq_rot = jnp.concatenate([_rotate_half_cols(uq_rope), z(HEAD_BLOCK - ROPE)], axis=-1)
    w_uq_ext = jnp.concatenate([uq_pad.reshape(Q_LORA, QK_WIDTH), uq_rot.reshape(Q_LORA, QK_WIDTH)], axis=1)

    ukv = w_ukv[l].reshape(KV_LORA, MLA_HEADS, NOPE + V_DIM)
    uk, uv = ukv[..., :NOPE], ukv[..., NOPE:]
    zk = lambda n: jnp.zeros((KV_LORA, MLA_HEADS, n), F32)
    kn_pad = jnp.concatenate([zk(ROPE), uk, zk(HEAD_BLOCK - ROPE - NOPE)], axis=-1)
    w_k = kn_pad.reshape(KV_LORA, QK_WIDTH)
    w_vt = uv.reshape(KV_LORA, BRANCH).T
    uk_t = jnp.transpose(uk, (1, 2, 0))
    w_uk_abs = jnp.concatenate([jnp.zeros((MLA_HEADS, ROPE, KV_LORA), F32), uk_t,
                                jnp.zeros((MLA_HEADS, HEAD_BLOCK - ROPE - NOPE, KV_LORA), F32)], axis=1)
    eye = jnp.eye(MLA_HEADS, dtype=F32)
    w_uv_abs = (jnp.transpose(uv, (1, 0, 2))[:, :, None, :] * eye[:, None, :, None]).reshape(
        MLA_HEADS, KV_LORA, BRANCH)

    g = norm_gains[l]
    return {
        "g0": row(g[0]), "g1": row(g[1]), "g2": row(g[2]), "g3": row(g[3]), "g4": row(g[4]), "g5": row(g[5]),
        "w_in": w_in_ext.astype(BF16),
        "a_v_norm": row(a_v_norm[l]), "a_w_s": a_w_s[l], "a_b_s_t": a_b_s[l].T,
        "a_w_s0": row(jnp.repeat(a_w_s[l][:, 0, 0], CHUNK)), "a_b_s0": row(jnp.repeat(a_b_s[l][:, 0], CHUNK)),
        "q_norm": row(q_norm[l]), "kv_norm": row(kv_norm[l]),
        "w_uq": w_uq_ext.astype(BF16), "w_k": w_k.astype(BF16), "w_vt": w_vt.astype(BF16),
        "w_uk_abs": w_uk_abs.astype(BF16), "w_uv_abs": w_uv_abs.astype(BF16),
        "conv_w": conv_w[l], "conv_b": row(conv_b[l]), "conv_ln_g": row(conv_ln_g[l]),
        "conv_ln_b": row(conv_ln_b[l]),
        "w_branch": w_branch[l].astype(BF16), "w_out": w_out[l].astype(BF16),
        "w_mq": w_mq[l].astype(BF16), "w_mo": w_mo[l].astype(BF16),
        "w_ff1": w_ff1[l].astype(BF16), "w_ff2": w_ff2[l].astype(BF16),
    }


def _row_tile(rows, want):
    t = min(rows, want)
    while rows % t:
        t //= 2
    return t


def kernel(x_prompt, x_sample, mem_prompt, cache_latent, cache_conv, cache_mem_k, cache_mem_v, page_table,
           norm_gains, mem_norm, w_in, a_v_norm, a_w_s, a_b_s, q_norm, w_uq, kv_norm, w_ukv, conv_w, conv_b,
           conv_ln_g, conv_ln_b, w_branch, w_out, w_mq, w_mk, w_mv, w_mo, w_ff1, w_ff2):
    depth = w_in.shape[0]
    n_seq, seq, _ = x_prompt.shape
    n_s, t_s, _ = x_sample.shape
    assert t_s == 1 and seq % CHUNK == 0 and seq >= CONV_HALO
    n_pages = page_table.shape[1]
    past = n_pages * PAGE

    tm_mix = _row_tile(seq, 512)
    tm_row = _row_tile(seq, 512)
    tq_attn = _row_tile(seq, 512)
    tk_attn = _row_tile(seq, 512)
    tm_s = n_s
    mem_group = _row_tile(n_s, 8)

    cos_p, sin_p = _rope_tables(jnp.arange(seq, dtype=jnp.int32))
    cos_s, sin_s = _rope_tables(jnp.full((n_s,), past, jnp.int32))

    xp = x_prompt.reshape(n_seq * seq, D_MODEL)
    xs = x_sample.reshape(n_s, D_MODEL)
    mem = mem_prompt.reshape(n_seq * N_MEM, D_MODEL)
    cache_k = jnp.transpose(cache_mem_k, (0, 1, 3, 4, 2)).reshape(depth, n_s, MEM_WIDTH, N_MEM)
    cache_v = jnp.transpose(cache_mem_v, (0, 1, 3, 4, 2)).reshape(depth, n_s, MEM_WIDTH, N_MEM)
    cache_lat = jnp.transpose(cache_latent, (0, 1, 3, 2))

    lat_p_l, lat_s_l, conv_p_l, conv_s_l, v_s_l, mk_p_l, mv_p_l = [], [], [], [], [], [], []
    for l in range(depth):
        lw = _layer_weights(l, norm_gains, w_in, a_v_norm, a_w_s, a_b_s, q_norm, w_uq, kv_norm, w_ukv,
                            conv_w, conv_b, conv_ln_g, conv_ln_b, w_branch, w_out, w_mq, w_mo, w_ff1, w_ff2)

        mk, mv = _memory_kv(mem, mem_norm[l].reshape(1, -1), w_mk[l].astype(BF16), w_mv[l].astype(BF16))
        out_a, q, lat, k, vt, out_c, gates, tail = _mix_prompt(xp, cos_p, sin_p, lw, tm=tm_mix)
        out_b = _attn_prompt(q, k, vt, seq=seq, tq=tq_attn, tk=tk_attn)
        x1, qm = _merge(xp, out_a, out_b, out_c, gates, lw, tm=tm_row)
        o_mem = _mem_attn_prompt(qm, mk.reshape(n_seq, N_MEM, MEM_WIDTH), mv.reshape(n_seq, N_MEM, MEM_WIDTH),
                                 seq=seq, tm=tm_row)
        xp = _mlp(x1, o_mem, lw, tm=tm_row)
        lat_p_l.append(lat.reshape(n_seq, seq, LATENT))
        conv_p_l.append(tail[:, CONV_HALO - (CONV_W - 1):, :])
        mk_p_l.append(mk.reshape(n_seq, N_MEM, MEM_HEADS, MEM_HEAD_DIM))
        mv_p_l.append(mv.reshape(n_seq, N_MEM, MEM_HEADS, MEM_HEAD_DIM))

        conv_t = jnp.transpose(cache_conv[l], (1, 0, 2))
        out_a, q, lat, out_c, gates, v_n, glu = _mix_sample(xs, cos_s, sin_s, lw, conv_t)
        q_lat, q_rope = _q_latent(q, lw["w_uk_abs"])
        o_lat = _attn_sample(page_table, jnp.transpose(q_lat, (1, 0, 2)), jnp.transpose(q_rope, (1, 0, 2)),
                             lat.reshape(n_s, 1, LATENT), cache_lat, layer=l)
        out_b = _v_up(jnp.transpose(o_lat, (1, 0, 2)), lw["w_uv_abs"])
        x1, qm = _merge(xs, out_a, out_b, out_c, gates, lw, tm=tm_s)
        o_mem = _mem_attn_sample(qm, cache_k, cache_v, layer=l, group=mem_group)
        xs = _mlp(x1, o_mem, lw, tm=tm_s)
        lat_s_l.append(lat.reshape(n_s, 1, LATENT))
        conv_s_l.append(jnp.concatenate([cache_conv[l][:, 1:, :], glu[:, None, :]], axis=1))
        v_s_l.append(v_n.reshape(n_s, 1, BRANCH))

    return (xp.reshape(n_seq, seq, D_MODEL), xs.reshape(n_s, 1, D_MODEL),
            jnp.stack(lat_p_l), jnp.stack(lat_s_l), jnp.stack(conv_p_l), jnp.stack(conv_s_l),
            jnp.stack(v_s_l), jnp.stack(mk_p_l), jnp.stack(mv_p_l))
```

```python
import functools

import jax
import jax.numpy as jnp
import numpy as np
from jax import lax
from jax.experimental import pallas as pl
from jax.experimental.pallas import tpu as pltpu

F32 = jnp.float32
BF16 = jnp.bfloat16

D_MODEL = 1024
BRANCH = 512
N_BRANCH = 3
CHUNK = 128
A_GROUPS = 4
MLA_HEADS = 8
Q_LORA = 256
KV_LORA = 256
NOPE = 64
ROPE = 32
V_DIM = 64
LATENT = KV_LORA + ROPE
ROPE_THETA = 10000.0
CONV_W = 31
N_MEM = 256
MEM_HEADS = 4
MEM_HEAD_DIM = 64
MEM_WIDTH = MEM_HEADS * MEM_HEAD_DIM
D_FF = 4 * D_MODEL
EPS = 1e-6
PAGE = 128

HEAD_BLOCK = 128
V_AUG = V_DIM + 16
QK_WIDTH = MLA_HEADS * HEAD_BLOCK
Q_SCALE = (NOPE + ROPE) ** -0.5
MEM_SCALE = MEM_HEAD_DIM ** -0.5
LOG2_E = float(np.log2(np.e))

COL_A = 0
COL_B = COL_A + 2 * BRANCH
COL_C = COL_B + Q_LORA + KV_LORA + 2 * HEAD_BLOCK
COL_G = COL_C + 2 * BRANCH
IN_EXT = COL_G + N_BRANCH * D_MODEL

CONV_HALO = 32
VMEM_LIMIT = 56 * 1024 * 1024


def _params(*sem):
    return pltpu.CompilerParams(dimension_semantics=sem, vmem_limit_bytes=VMEM_LIMIT)


def _const_spec(shape):
    n = len(shape)
    return pl.BlockSpec(shape, lambda *_: (0,) * n, pipeline_mode=pl.Buffered(1))


def _dot(a, b):
    return jnp.dot(a, b, preferred_element_type=F32)


def _dot_nt(a, b):
    return lax.dot_general(a, b, (((1,), (1,)), ((), ())), preferred_element_type=F32)


def _rms(x, g):
    return x * lax.rsqrt(jnp.mean(x * x, axis=-1, keepdims=True) + EPS) * g


def _gelu(x):
    c = np.sqrt(2.0 / np.pi).astype(np.float32)
    return x * (0.5 * (1.0 + jnp.tanh(c * (x + 0.044715 * (x * x * x)))))


def _sigmoid(x):
    return 1.0 / (1.0 + jnp.exp(-x))


def _silu_layer_norm(x, g, b):
    mu = jnp.mean(x, axis=-1, keepdims=True)
    xc = x - mu
    var = jnp.mean(xc * xc, axis=-1, keepdims=True)
    y = xc * lax.rsqrt(var + EPS) * g + b
    return y * _sigmoid(y)


def _mix_front(x_ref, g0_ref, w_in_ref, cos_ref, sin_ref, avn_ref, qn_ref, kvn_ref, w_uq_ref,
               q_ref, lat_ref, gates_ref, q_scale, lat_transposed):
    h = _rms(x_ref[...], g0_ref[...]).astype(BF16)

    za = _dot_nt(h, w_in_ref[COL_A:COL_B, :])
    u = _gelu(za[:, :BRANCH])
    v_n = _rms(_gelu(za[:, BRANCH:]), avn_ref[...])

    zb = _dot_nt(h, w_in_ref[COL_B:COL_C, :])
    c_q = _rms(zb[:, :Q_LORA], qn_ref[...])
    c_kv = _rms(zb[:, Q_LORA:Q_LORA + KV_LORA], kvn_ref[...])
    cos = cos_ref[...]
    sin = sin_ref[...]
    o = Q_LORA + KV_LORA
    k_rope = zb[:, o:o + HEAD_BLOCK] * cos + zb[:, o + HEAD_BLOCK:o + 2 * HEAD_BLOCK] * sin

    qq = _dot(c_q.astype(BF16), w_uq_ref[...]) * q_scale
    for hd in range(MLA_HEADS):
        lo = hd * HEAD_BLOCK
        q_h = qq[:, lo:lo + HEAD_BLOCK] * cos + qq[:, QK_WIDTH + lo:QK_WIDTH + lo + HEAD_BLOCK] * sin
        q_ref[:, lo:lo + HEAD_BLOCK] = q_h.astype(q_ref.dtype)

    if lat_transposed:
        lat_ref[:KV_LORA, :] = c_kv.T
        lat_ref[KV_LORA:, :] = k_rope.T[:ROPE, :]
    else:
        lat_ref[:, :KV_LORA] = c_kv
        lat_ref[:, KV_LORA:] = k_rope[:, :ROPE]

    zc = _dot_nt(h, w_in_ref[COL_C:COL_G, :])
    glu = zc[:, :BRANCH] * _sigmoid(zc[:, BRANCH:])

    for k in range(N_BRANCH):
        lo = COL_G + k * D_MODEL
        gates_ref[:, k * D_MODEL:(k + 1) * D_MODEL] = _sigmoid(_dot_nt(h, w_in_ref[lo:lo + D_MODEL, :]))
    return u, v_n, c_kv, k_rope, glu


def _mix_prompt_kernel(x_ref, cos_ref, sin_ref, g0_ref, w_in_ref, avn_ref, ws_ref, bs_ref, qn_ref,
                       kvn_ref, w_uq_ref, w_k_ref, w_vt_ref, cw_ref, cb_ref, lng_ref, lnb_ref,
                       outa_ref, q_ref, lat_ref, k_ref, vt_ref, outc_ref, gates_ref, tail_ref,
                       xp_ref, sh_ref, *, tm, tiles_per_seq):
    u, v_n, c_kv, k_rope, glu = _mix_front(x_ref, g0_ref, w_in_ref, cos_ref, sin_ref, avn_ref,
                                           qn_ref, kvn_ref, w_uq_ref, q_ref, lat_ref, gates_ref,
                                           Q_SCALE * LOG2_E, True)

    row = lax.broadcasted_iota(jnp.int32, (CHUNK, CHUNK), 0)
    col = lax.broadcasted_iota(jnp.int32, (CHUNK, CHUNK), 1)
    causal = col <= row
    for g in range(A_GROUPS):
        w_g = jnp.where(causal, ws_ref[g], 0.0).astype(BF16)
        b_g = bs_ref[:, g:g + 1]
        for c in range(tm // CHUNK):
            rs = slice(c * CHUNK, (c + 1) * CHUNK)
            cs = slice(g * CHUNK, (g + 1) * CHUNK)
            mixed = _dot(w_g, v_n[rs, cs].astype(BF16)) + b_g
            outa_ref[rs, cs] = (u[rs, cs] * mixed).astype(outa_ref.dtype)

    c_kv16 = c_kv.astype(BF16)
    k_nope = _dot(c_kv16, w_k_ref[...])
    for hd in range(MLA_HEADS):
        lo = hd * HEAD_BLOCK
        k_ref[:, lo:lo + HEAD_BLOCK] = (k_nope[:, lo:lo + HEAD_BLOCK] + k_rope).astype(k_ref.dtype)
    v_t = _dot_nt(w_vt_ref[...], c_kv16).astype(vt_ref.dtype)
    ones_rows = jnp.ones((V_AUG - V_DIM, tm), vt_ref.dtype)
    for hd in range(MLA_HEADS):
        vt_ref[hd * V_AUG:hd * V_AUG + V_DIM, :] = v_t[hd * V_DIM:(hd + 1) * V_DIM, :]
        vt_ref[hd * V_AUG + V_DIM:(hd + 1) * V_AUG, :] = ones_rows

    first = (pl.program_id(0) % tiles_per_seq) == 0

    @pl.when(first)
    def _():
        xp_ref[:CONV_HALO, :] = jnp.zeros((CONV_HALO, BRANCH), F32)

    @pl.when(jnp.logical_not(first))
    def _():
        xp_ref[:CONV_HALO, :] = xp_ref[tm:tm + CONV_HALO, :]

    xp_ref[CONV_HALO:, :] = glu
    acc = jnp.zeros((tm, BRANCH), F32) + cb_ref[...]
    off = CONV_HALO - (CONV_W - 1)
    for phase in range(8):
        taps = [j for j in range(CONV_W) if (off + j) % 8 == phase]
        if not taps:
            continue
        base = min(off + j for j in taps)
        span = max(off + j for j in taps) - base + tm
        if phase == 0:
            src_ref, src_base = xp_ref, base
        else:
            sh_ref[:span, :] = xp_ref[base:base + span, :]
            src_ref, src_base = sh_ref, 0
        for j in taps:
            lo = src_base + off + j - base
            acc = acc + cw_ref[j:j + 1, :] * src_ref[lo:lo + tm, :]
    outc_ref[...] = _silu_layer_norm(acc, lng_ref[...], lnb_ref[...]).astype(outc_ref.dtype)
    tail_ref[...] = xp_ref[tm:tm + CONV_HALO, :]


def _mix_sample_kernel(x_ref, cos_ref, sin_ref, g0_ref, w_in_ref, avn_ref, ws0_ref, bs0_ref, qn_ref,
                       kvn_ref, w_uq_ref, cw_ref, cb_ref, lng_ref, lnb_ref, conv_ref,
                       outa_ref, q_ref, lat_ref, outc_ref, gates_ref, vn_ref, glu_ref):
    u, v_n, _, _, glu = _mix_front(x_ref, g0_ref, w_in_ref, cos_ref, sin_ref, avn_ref,
                                   qn_ref, kvn_ref, w_uq_ref, q_ref, lat_ref, gates_ref, Q_SCALE, False)
    vn_ref[...] = v_n
    glu_ref[...] = glu
    outa_ref[...] = (u * (ws0_ref[...] * v_n + bs0_ref[...])).astype(outa_ref.dtype)
    acc = cb_ref[...] + cw_ref[CONV_W - 1:CONV_W, :] * glu
    for j in range(CONV_W - 1):
        acc = acc + cw_ref[j:j + 1, :] * conv_ref[j]
    outc_ref[...] = _silu_layer_norm(acc, lng_ref[...], lnb_ref[...]).astype(outc_ref.dtype)


def _mix_prompt(x, cos, sin, lw, *, tm):
    rows = x.shape[0]
    seq = cos.shape[0]
    tiles_per_seq = seq // tm
    n_seq = rows // seq
    grid = (rows // tm,)
    row_spec = lambda w: pl.BlockSpec((tm, w), lambda i: (i, 0))
    pos_spec = pl.BlockSpec((tm, HEAD_BLOCK), lambda i: (i % tiles_per_seq, 0))
    in_specs = [
        row_spec(D_MODEL), pos_spec, pos_spec,
        _const_spec((1, D_MODEL)), _const_spec((IN_EXT, D_MODEL)), _const_spec((1, BRANCH)),
        _const_spec((A_GROUPS, CHUNK, CHUNK)), _const_spec((CHUNK, A_GROUPS)),
        _const_spec((1, Q_LORA)), _const_spec((1, KV_LORA)),
        _const_spec((Q_LORA, 2 * QK_WIDTH)), _const_spec((KV_LORA, QK_WIDTH)),
        _const_spec((BRANCH, KV_LORA)),
        _const_spec((CONV_W, BRANCH)), _const_spec((1, BRANCH)), _const_spec((1, BRANCH)),
        _const_spec((1, BRANCH)),
    ]
    out_shape = (
        jax.ShapeDtypeStruct((rows, BRANCH), BF16),
        jax.ShapeDtypeStruct((rows, QK_WIDTH), BF16),
        jax.ShapeDtypeStruct((n_seq, LATENT, seq), F32),
        jax.ShapeDtypeStruct((rows, QK_WIDTH), BF16),
        jax.ShapeDtypeStruct((n_seq, tiles_per_seq, MLA_HEADS * V_AUG, tm), BF16),
        jax.ShapeDtypeStruct((rows, BRANCH), BF16),
        jax.ShapeDtypeStruct((rows, N_BRANCH * D_MODEL), F32),
        jax.ShapeDtypeStruct((n_seq, CONV_HALO, BRANCH), F32),
    )
    out_specs = (
        row_spec(BRANCH), row_spec(QK_WIDTH),
        pl.BlockSpec((None, LATENT, tm), lambda i: (i // tiles_per_seq, 0, i % tiles_per_seq)),
        row_spec(QK_WIDTH),
        pl.BlockSpec((None, None, MLA_HEADS * V_AUG, tm),
                     lambda i: (i // tiles_per_seq, i % tiles_per_seq, 0, 0)),
        row_spec(BRANCH), row_spec(N_BRANCH * D_MODEL),
        pl.BlockSpec((None, CONV_HALO, BRANCH), lambda i: (i // tiles_per_seq, 0, 0)),
    )
    return pl.pallas_call(
        functools.partial(_mix_prompt_kernel, tm=tm, tiles_per_seq=tiles_per_seq),
        grid=grid, in_specs=in_specs, out_specs=out_specs, out_shape=out_shape,
        scratch_shapes=[pltpu.VMEM((tm + CONV_HALO, BRANCH), F32)] * 2,
        compiler_params=_params("arbitrary"), name="mix_prompt",
    )(x, cos, sin, lw["g0"], lw["w_in"], lw["a_v_norm"], lw["a_w_s"], lw["a_b_s_t"], lw["q_norm"],
      lw["kv_norm"], lw["w_uq"], lw["w_k"], lw["w_vt"], lw["conv_w"], lw["conv_b"], lw["conv_ln_g"],
      lw["conv_ln_b"])


def _mix_sample(x, cos, sin, lw, conv_t):
    rows = x.shape[0]
    full = lambda shape: pl.BlockSpec(shape, lambda i: (0,) * len(shape))
    in_specs = [
        full((rows, D_MODEL)), full((rows, HEAD_BLOCK)), full((rows, HEAD_BLOCK)),
        full((1, D_MODEL)), _const_spec((IN_EXT, D_MODEL)), full((1, BRANCH)),
        full((1, BRANCH)), full((1, BRANCH)), full((1, Q_LORA)), full((1, KV_LORA)),
        full((Q_LORA, 2 * QK_WIDTH)), full((CONV_W, BRANCH)), full((1, BRANCH)), full((1, BRANCH)),
        full((1, BRANCH)), full((CONV_W - 1, rows, BRANCH)),
    ]
    out_shape = (
        jax.ShapeDtypeStruct((rows, BRANCH), BF16),
        jax.ShapeDtypeStruct((rows, QK_WIDTH), F32),
        jax.ShapeDtypeStruct((rows, LATENT), F32),
        jax.ShapeDtypeStruct((rows, BRANCH), BF16),
        jax.ShapeDtypeStruct((rows, N_BRANCH * D_MODEL), F32),
        jax.ShapeDtypeStruct((rows, BRANCH), F32),
        jax.ShapeDtypeStruct((rows, BRANCH), F32),
    )
    out_specs = tuple(full(s.shape) for s in out_shape)
    return pl.pallas_call(
        _mix_sample_kernel, grid=(1,), in_specs=in_specs, out_specs=out_specs, out_shape=out_shape,
        compiler_params=_params("arbitrary"), name="mix_sample",
    )(x, cos, sin, lw["g0"], lw["w_in"], lw["a_v_norm"], lw["a_w_s0"], lw["a_b_s0"], lw["q_norm"],
      lw["kv_norm"], lw["w_uq"], lw["conv_w"], lw["conv_b"], lw["conv_ln_g"], lw["conv_ln_b"], conv_t)


def _attn_prompt_kernel(qi_ref, kj_ref, q_ref, k_ref, vt_ref, o_ref, m_ref, acc_ref, *, t):
    step = pl.program_id(1)
    qi = qi_ref[step]
    kj = kj_ref[step]

    @pl.when(kj == 0)
    def _():
        m_ref[...] = jnp.full(m_ref.shape, -jnp.inf, F32)
        acc_ref[...] = jnp.zeros(acc_ref.shape, F32)

    def update(sub, masked):
        if masked:
            key = lax.broadcasted_iota(jnp.int32, (t, t), 0)
            qry = lax.broadcasted_iota(jnp.int32, (t, t), 1)
            keep = key <= qry
        rows = slice(sub * t, (sub + 1) * t)

        def scores(hd):
            lo = hd * HEAD_BLOCK
            return _dot_nt(k_ref[rows, lo:lo + HEAD_BLOCK], q_ref[:, lo:lo + HEAD_BLOCK])

        ahead = 2
        pending = [scores(hd) for hd in range(ahead)]
        for hd in range(MLA_HEADS):
            vs = slice(hd * V_DIM, (hd + 1) * V_DIM)
            s = pending.pop(0)
            if hd + ahead < MLA_HEADS:
                pending.append(scores(hd + ahead))
            if masked:
                s = jnp.where(keep, s, -jnp.inf)
            m_old = m_ref[hd]
            m_new = jnp.maximum(m_old, jnp.max(s, axis=0, keepdims=True))
            alpha = jnp.exp2(m_old - m_new)
            p = jnp.exp2(s - m_new)
            m_ref[hd] = m_new
            v_aug = vt_ref[sub, hd * V_AUG:(hd + 1) * V_AUG, :]
            acc_ref[hd] = acc_ref[hd] * alpha + _dot(v_aug, p.astype(BF16))

    def finalize():
        outs = []
        for hd in range(MLA_HEADS):
            acc = acc_ref[hd]
            outs.append(acc[:V_DIM, :] / acc[V_DIM:V_DIM + 1, :])
        o_ref[...] = jnp.concatenate(outs, axis=0).T.astype(o_ref.dtype)

    last_pair = 2 * kj + 1 >= qi

    @pl.when(jnp.logical_not(last_pair))
    def _():
        update(0, False)
        update(1, False)

    @pl.when(jnp.logical_and(last_pair, qi % 2 == 0))
    def _():
        update(0, True)
        finalize()

    @pl.when(jnp.logical_and(last_pair, qi % 2 == 1))
    def _():
        update(0, False)
        update(1, True)
        finalize()


def _attn_prompt(q, k, vt, *, seq, t):
    rows = q.shape[0]
    n_seq = rows // seq
    nq = seq // t
    n_pairs = nq // 2
    assert nq % 2 == 0 and vt.shape == (n_seq, nq, MLA_HEADS * V_AUG, t), (nq, vt.shape)
    steps = [(i, j) for i in range(nq) for j in range(i // 2 + 1)]
    qi_tab = np.array([i for i, _ in steps], np.int32)
    kj_tab = np.array([j for _, j in steps], np.int32)
    grid_spec = pltpu.PrefetchScalarGridSpec(
        num_scalar_prefetch=2,
        grid=(n_seq, len(steps)),
        in_specs=[
            pl.BlockSpec((t, QK_WIDTH), lambda b, s, qi, kj: (b * nq + qi[s], 0)),
            pl.BlockSpec((2 * t, QK_WIDTH), lambda b, s, qi, kj: (b * n_pairs + kj[s], 0)),
            pl.BlockSpec((None, 2, MLA_HEADS * V_AUG, t), lambda b, s, qi, kj: (b, kj[s], 0, 0)),
        ],
        out_specs=pl.BlockSpec((t, BRANCH), lambda b, s, qi, kj: (b * nq + qi[s], 0)),
        scratch_shapes=[
            pltpu.VMEM((MLA_HEADS, 1, t), F32),
            pltpu.VMEM((MLA_HEADS, V_AUG, t), F32),
        ],
    )
    return pl.pallas_call(
        functools.partial(_attn_prompt_kernel, t=t),
        grid_spec=grid_spec, out_shape=jax.ShapeDtypeStruct((rows, BRANCH), BF16),
        compiler_params=_params("arbitrary", "arbitrary"), name="attn_prompt",
    )(jnp.asarray(qi_tab), jnp.asarray(kj_tab), q, k, vt)


def _q_latent_kernel(q_ref, w_uk_ref, qlat_ref, qrope_ref):
    lane = lax.broadcasted_iota(jnp.int32, (q_ref.shape[0], HEAD_BLOCK), 1)
    for hd in range(MLA_HEADS):
        q_h = q_ref[:, hd * HEAD_BLOCK:(hd + 1) * HEAD_BLOCK]
        qlat_ref[hd] = _dot(q_h.astype(BF16), w_uk_ref[hd])
        qrope_ref[hd] = jnp.where(lane < ROPE, q_h, 0.0)


def _q_latent(q, w_uk):
    rows = q.shape[0]
    full = lambda shape: pl.BlockSpec(shape, lambda i: (0,) * len(shape))
    return pl.pallas_call(
        _q_latent_kernel, grid=(1,),
        in_specs=[full((rows, QK_WIDTH)), full((MLA_HEADS, HEAD_BLOCK, KV_LORA))],
        out_specs=(full((MLA_HEADS, rows, KV_LORA)), full((MLA_HEADS, rows, HEAD_BLOCK))),
        out_shape=(jax.ShapeDtypeStruct((MLA_HEADS, rows, KV_LORA), F32),
                   jax.ShapeDtypeStruct((MLA_HEADS, rows, HEAD_BLOCK), F32)),
        compiler_params=_params("arbitrary"), name="q_latent",
    )(q, w_uk)


def _attn_sample_kernel(pt_ref, qlat_ref, qrope_ref, latn_ref, cache_ref, o_ref, buf_ref, sem_ref, *,
                        layer, n_pages, key_chunk):
    b = pl.program_id(0)
    nb = pl.num_programs(0)
    past = n_pages * PAGE

    def page_copy(sample, page, slot):
        return pltpu.make_async_copy(
            cache_ref.at[layer, pt_ref[sample, page]],
            buf_ref.at[slot, :, pl.ds(pl.multiple_of(page * PAGE, PAGE), PAGE)],
            sem_ref.at[slot])

    def start_all(sample, slot):
        def body(page, carry):
            page_copy(sample, page, slot).start()
            return carry
        lax.fori_loop(0, n_pages, body, 0)

    def wait_all(sample, slot):
        def body(page, carry):
            page_copy(sample, page, slot).wait()
            return carry
        lax.fori_loop(0, n_pages, body, 0)

    slot = b % 2

    @pl.when(b == 0)
    def _():
        start_all(0, 0)

    @pl.when(b + 1 < nb)
    def _():
        start_all(b + 1, 1 - slot)

    wait_all(b, slot)

    q_lat = qlat_ref[...].astype(BF16)
    q_rope = qrope_ref[:, :ROPE].astype(BF16)
    lat_new = latn_ref[...].astype(BF16).astype(F32)
    s_new = (jnp.sum(q_lat.astype(F32) * lat_new[:, :KV_LORA], axis=-1, keepdims=True)
             + jnp.sum(q_rope.astype(F32) * lat_new[:, KV_LORA:], axis=-1, keepdims=True))

    n_chunks = past // key_chunk
    scores = []
    for c in range(n_chunks):
        cols = buf_ref[slot, :, c * key_chunk:(c + 1) * key_chunk].astype(BF16)
        scores.append(_dot(q_lat, cols[:KV_LORA, :]) + _dot(q_rope, cols[KV_LORA:, :]))
    m = s_new
    for s in scores:
        m = jnp.maximum(m, jnp.max(s, axis=-1, keepdims=True))
    exps = [jnp.exp(s - m) for s in scores]
    e_new = jnp.exp(s_new - m)
    denom = e_new
    for e in exps:
        denom = denom + jnp.sum(e, axis=-1, keepdims=True)
    p_new = (e_new / denom).astype(BF16).astype(F32)
    out = p_new * lat_new[:, :KV_LORA]
    for c in range(n_chunks):
        p = (exps[c] / denom).astype(BF16)
        cols = buf_ref[slot, :KV_LORA, c * key_chunk:(c + 1) * key_chunk].astype(BF16)
        out = out + _dot_nt(p, cols)
    o_ref[...] = out


def _attn_sample(page_table, q_lat, q_rope, lat_new, cache_latent, *, layer):
    n_s, n_pages = page_table.shape
    past = n_pages * PAGE
    key_chunk = 1024 if past % 1024 == 0 else PAGE
    grid_spec = pltpu.PrefetchScalarGridSpec(
        num_scalar_prefetch=1,
        grid=(n_s,),
        in_specs=[
            pl.BlockSpec((None, MLA_HEADS, KV_LORA), lambda b, pt: (b, 0, 0)),
            pl.BlockSpec((None, MLA_HEADS, HEAD_BLOCK), lambda b, pt: (b, 0, 0)),
            pl.BlockSpec((None, 1, LATENT), lambda b, pt: (b, 0, 0)),
            pl.BlockSpec(memory_space=pl.ANY),
        ],
        out_specs=pl.BlockSpec((None, MLA_HEADS, KV_LORA), lambda b, pt: (b, 0, 0)),
        scratch_shapes=[
            pltpu.VMEM((2, LATENT, past), F32),
            pltpu.SemaphoreType.DMA((2,)),
        ],
    )
    return pl.pallas_call(
        functools.partial(_attn_sample_kernel, layer=layer, n_pages=n_pages, key_chunk=key_chunk),
        grid_spec=grid_spec, out_shape=jax.ShapeDtypeStruct((n_s, MLA_HEADS, KV_LORA), F32),
        compiler_params=_params("arbitrary"), name="attn_sample",
    )(page_table, q_lat, q_rope, lat_new, cache_latent)


def _v_up_kernel(o_ref, w_uv_ref, out_ref):
    acc = _dot(o_ref[0].astype(BF16), w_uv_ref[0])
    for hd in range(1, MLA_HEADS):
        acc = acc + _dot(o_ref[hd].astype(BF16), w_uv_ref[hd])
    out_ref[...] = acc.astype(out_ref.dtype)


def _v_up(o_lat, w_uv):
    rows = o_lat.shape[1]
    full = lambda shape: pl.BlockSpec(shape, lambda i: (0,) * len(shape))
    return pl.pallas_call(
        _v_up_kernel, grid=(1,),
        in_specs=[full((MLA_HEADS, rows, KV_LORA)), full((MLA_HEADS, KV_LORA, BRANCH))],
        out_specs=full((rows, BRANCH)),
        out_shape=jax.ShapeDtypeStruct((rows, BRANCH), BF16),
        compiler_params=_params("arbitrary"), name="v_up",
    )(o_lat, w_uv)


def _merge_math(x_ref, a_ref, b_ref, c_ref, gates_ref, wb_ref, wo_ref, g1_ref, g2_ref, wmq_ref):
    merged = None
    for k, br_ref in enumerate((a_ref, b_ref, c_ref)):
        term = gates_ref[:, k * D_MODEL:(k + 1) * D_MODEL] * _dot(br_ref[...], wb_ref[k])
        merged = term if merged is None else merged + term
    y = _dot(merged.astype(BF16), wo_ref[...])
    x1 = x_ref[...] + _rms(y, g1_ref[...])
    h = _rms(x1, g2_ref[...]).astype(BF16)
    return x1, _dot(h, wmq_ref[...]) * MEM_SCALE


def _merge_kernel(x_ref, a_ref, b_ref, c_ref, gates_ref, wb_ref, wo_ref, g1_ref, g2_ref, wmq_ref,
                  x1_ref, qm_ref):
    x1_ref[...], qm_ref[...] = _merge_math(x_ref, a_ref, b_ref, c_ref, gates_ref, wb_ref, wo_ref,
                                           g1_ref, g2_ref, wmq_ref)


def _merge(x, out_a, out_b, out_c, gates, lw, *, tm):
    rows = x.shape[0]
    row_spec = lambda w: pl.BlockSpec((tm, w), lambda i: (i, 0))
    return pl.pallas_call(
        _merge_kernel, grid=(rows // tm,),
        in_specs=[row_spec(D_MODEL), row_spec(BRANCH), row_spec(BRANCH), row_spec(BRANCH),
                  row_spec(N_BRANCH * D_MODEL), _const_spec((N_BRANCH, BRANCH, D_MODEL)),
                  _const_spec((D_MODEL, D_MODEL)), _const_spec((1, D_MODEL)), _const_spec((1, D_MODEL)),
                  _const_spec((D_MODEL, MEM_WIDTH))],
        out_specs=(row_spec(D_MODEL), row_spec(MEM_WIDTH)),
        out_shape=(jax.ShapeDtypeStruct((rows, D_MODEL), F32),
                   jax.ShapeDtypeStruct((rows, MEM_WIDTH), F32)),
        compiler_params=_params("arbitrary"), name="merge",
    )(x, out_a, out_b, out_c, gates, lw["w_branch"], lw["w_out"], lw["g1"], lw["g2"], lw["w_mq"])


def _memory_kv_kernel(mem_ref, g_ref, wk_ref, wv_ref, k_ref, v_ref):
    m = _rms(mem_ref[...], g_ref[...]).astype(BF16)
    k_ref[...] = _dot(m, wk_ref[...])
    v_ref[...] = _dot(m, wv_ref[...])


def _memory_kv(mem, g, w_mk, w_mv):
    rows = mem.shape[0]
    full = lambda shape: pl.BlockSpec(shape, lambda i: (0,) * len(shape))
    return pl.pallas_call(
        _memory_kv_kernel, grid=(1,),
        in_specs=[full((rows, D_MODEL)), full((1, D_MODEL)), full((D_MODEL, MEM_WIDTH)),
                  full((D_MODEL, MEM_WIDTH))],
        out_specs=(full((rows, MEM_WIDTH)), full((rows, MEM_WIDTH))),
        out_shape=(jax.ShapeDtypeStruct((rows, MEM_WIDTH), F32),) * 2,
        compiler_params=_params("arbitrary"), name="memory_kv",
    )(mem, g, w_mk, w_mv)


def _head_lane_mask(rows):
    lane = lax.broadcasted_iota(jnp.int32, (rows, MEM_WIDTH), 1)
    return [(lane >= hd * MEM_HEAD_DIM) & (lane < (hd + 1) * MEM_HEAD_DIM) for hd in range(MEM_HEADS)]


def _mem_attn_math(qm, k_ref, v_ref):
    k = k_ref[...].astype(BF16)
    v = v_ref[...].astype(BF16)
    masks = _head_lane_mask(qm.shape[0])
    out = jnp.zeros(qm.shape, F32)
    for hd in range(MEM_HEADS):
        s = _dot_nt(jnp.where(masks[hd], qm, 0.0).astype(BF16), k)
        e = jnp.exp(s - jnp.max(s, axis=-1, keepdims=True))
        p = (e / jnp.sum(e, axis=-1, keepdims=True)).astype(BF16)
        out = out + jnp.where(masks[hd], _dot(p, v), 0.0)
    return out.astype(BF16)


def _mem_attn_prompt_kernel(qm_ref, k_ref, v_ref, o_ref):
    o_ref[...] = _mem_attn_math(qm_ref[...], k_ref, v_ref)


def _mem_attn_prompt(qm, mem_k, mem_v, *, seq, tm):
    rows = qm.shape[0]
    tiles_per_seq = seq // tm
    kv_spec = pl.BlockSpec((None, N_MEM, MEM_WIDTH), lambda i: (i // tiles_per_seq, 0, 0))
    return pl.pallas_call(
        _mem_attn_prompt_kernel, grid=(rows // tm,),
        in_specs=[pl.BlockSpec((tm, MEM_WIDTH), lambda i: (i, 0)), kv_spec, kv_spec],
        out_specs=pl.BlockSpec((tm, MEM_WIDTH), lambda i: (i, 0)),
        out_shape=jax.ShapeDtypeStruct((rows, MEM_WIDTH), BF16),
        compiler_params=_params("arbitrary"), name="mem_attn_prompt",
    )(qm, mem_k, mem_v)


def _mem_attn_sample_kernel(qm_ref, kt_ref, vt_ref, o_ref, *, group):
    sub = lax.broadcasted_iota(jnp.int32, (8, MEM_WIDTH), 0)
    lane = lax.broadcasted_iota(jnp.int32, (8, MEM_WIDTH), 1)
    own = (lane >= sub * MEM_HEAD_DIM) & (lane < (sub + 1) * MEM_HEAD_DIM)
    for i in range(group):
        q_rows = jnp.where(own, qm_ref[i:i + 1, :], 0.0).astype(BF16)
        s = _dot(q_rows, kt_ref[i].astype(BF16))
        e = jnp.exp(s - jnp.max(s, axis=-1, keepdims=True))
        p = (e / jnp.sum(e, axis=-1, keepdims=True)).astype(BF16)
        o_all = _dot_nt(p, vt_ref[i].astype(BF16))
        o_ref[i:i + 1, :] = jnp.sum(jnp.where(own, o_all, 0.0), axis=0, keepdims=True).astype(o_ref.dtype)


def _mem_attn_sample(qm, cache_k, cache_v, *, layer, group):
    rows = qm.shape[0]
    kv_spec = pl.BlockSpec((None, group, MEM_WIDTH, N_MEM), lambda i: (layer, i, 0, 0))
    return pl.pallas_call(
        functools.partial(_mem_attn_sample_kernel, group=group), grid=(rows // group,),
        in_specs=[pl.BlockSpec((group, MEM_WIDTH), lambda i: (i, 0)), kv_spec, kv_spec],
        out_specs=pl.BlockSpec((group, MEM_WIDTH), lambda i: (i, 0)),
        out_shape=jax.ShapeDtypeStruct((rows, MEM_WIDTH), F32),
        compiler_params=_params("arbitrary"), name="mem_attn_sample",
    )(qm, cache_k, cache_v)


FF_CHUNK = 1024


def _mlp_math(x1, o, wmo_ref, g3_ref, g4_ref, w1_ref, w2_ref, g5_ref):
    x2 = x1 + _rms(_dot(o.astype(BF16), wmo_ref[...]), g3_ref[...])
    h = _rms(x2, g4_ref[...]).astype(BF16)
    f = None
    for c in range(D_FF // FF_CHUNK):
        cs = slice(c * FF_CHUNK, (c + 1) * FF_CHUNK)
        a = jnp.maximum(_dot(h, w1_ref[:, cs]), 0.0)
        part = _dot((a * a).astype(BF16), w2_ref[cs, :])
        f = part if f is None else f + part
    return x2 + _rms(f, g5_ref[...])


def _mlp_kernel(x1_ref, o_ref, wmo_ref, g3_ref, g4_ref, w1_ref, w2_ref, g5_ref, x3_ref):
    x3_ref[...] = _mlp_math(x1_ref[...], o_ref[...], wmo_ref, g3_ref, g4_ref, w1_ref, w2_ref, g5_ref)


def _mlp(x1, o, lw, *, tm):
    rows = x1.shape[0]
    row_spec = lambda w: pl.BlockSpec((tm, w), lambda i: (i, 0))
    return pl.pallas_call(
        _mlp_kernel, grid=(rows // tm,),
        in_specs=[row_spec(D_MODEL), row_spec(MEM_WIDTH), _const_spec((MEM_WIDTH, D_MODEL)),
                  _const_spec((1, D_MODEL)), _const_spec((1, D_MODEL)), _const_spec((D_MODEL, D_FF)),
                  _const_spec((D_FF, D_MODEL)), _const_spec((1, D_MODEL))],
        out_specs=row_spec(D_MODEL),
        out_shape=jax.ShapeDtypeStruct((rows, D_MODEL), F32),
        compiler_params=_params("arbitrary"), name="mlp",
    )(x1, o, lw["w_mo"], lw["g3"], lw["g4"], lw["w_ff1"], lw["w_ff2"], lw["g5"])


def _rotate_half_cols(w):
    half = ROPE // 2
    return jnp.concatenate([-w[..., half:], w[..., :half]], axis=-1)


def _rope_tables(pos):
    half = ROPE // 2
    inv = 1.0 / (ROPE_THETA ** (jnp.arange(half, dtype=F32) / half))
    ang = pos.astype(F32)[:, None] * inv[None, :]
    cos, sin = jnp.cos(ang), jnp.sin(ang)
    n = pos.shape[0]
    cos_blk = jnp.concatenate([cos, cos, jnp.ones((n, HEAD_BLOCK - ROPE), F32)], axis=1)
    sin_blk = jnp.concatenate([sin, sin, jnp.zeros((n, HEAD_BLOCK - ROPE), F32)], axis=1)
    return cos_blk, sin_blk


def _layer_weights(l, norm_gains, w_in, a_v_norm, a_w_s, a_b_s, q_norm, w_uq, kv_norm, w_ukv, conv_w,
                   conv_b, conv_ln_g, conv_ln_b, w_branch, w_out, w_mq, w_mo, w_ff1, w_ff2):
    row = lambda v: v.reshape(1, -1)
    wit = w_in[l].astype(BF16).T
    o_k = 2 * BRANCH + Q_LORA + KV_LORA
    o_c = o_k + ROPE
    kr_t = wit[o_k:o_c]
    pad_t = jnp.zeros((HEAD_BLOCK - ROPE, D_MODEL), BF16)
    w_in_t = jnp.concatenate([wit[:o_k], kr_t, pad_t, _rotate_half_cols(kr_t.T).T, pad_t, wit[o_c:]], axis=0)

    uq = w_uq[l].reshape(Q_LORA, MLA_HEADS, NOPE + ROPE)
    uq_nope, uq_rope = uq[..., :NOPE], uq[..., NOPE:]
    z = lambda n: jnp.zeros((Q_LORA, MLA_HEADS, n), F32)
    uq_pad = jnp.concatenate([uq_rope, uq_nope, z(HEAD_BLOCK - ROPE - NOPE)], axis=-1)
    uq_rot = jnp.concatenate([_rotate_half_cols(uq_rope), z(HEAD_BLOCK - ROPE)], axis=-1)
    w_uq_ext = jnp.concatenate([uq_pad.reshape(Q_LORA, QK_WIDTH), uq_rot.reshape(Q_LORA, QK_WIDTH)], axis=1)

    ukv = w_ukv[l].reshape(KV_LORA, MLA_HEADS, NOPE + V_DIM)
    uk, uv = ukv[..., :NOPE], ukv[..., NOPE:]
    zk = lambda n: jnp.zeros((KV_LORA, MLA_HEADS, n), F32)
    kn_pad = jnp.concatenate([zk(ROPE), uk, zk(HEAD_BLOCK - ROPE - NOPE)], axis=-1)
    w_k = kn_pad.reshape(KV_LORA, QK_WIDTH)
    w_vt = uv.reshape(KV_LORA, BRANCH).T
    uk_t = jnp.transpose(uk, (1, 2, 0))
    w_uk_abs = jnp.concatenate([jnp.zeros((MLA_HEADS, ROPE, KV_LORA), F32), uk_t,
                                jnp.zeros((MLA_HEADS, HEAD_BLOCK - ROPE - NOPE, KV_LORA), F32)], axis=1)
    eye = jnp.eye(MLA_HEADS, dtype=F32)
    w_uv_abs = (jnp.transpose(uv, (1, 0, 2))[:, :, None, :] * eye[:, None, :, None]).reshape(
        MLA_HEADS, KV_LORA, BRANCH)

    g = norm_gains[l]
    return {
        "g0": row(g[0]), "g1": row(g[1]), "g2": row(g[2]), "g3": row(g[3]), "g4": row(g[4]), "g5": row(g[5]),
        "w_in": w_in_t,
        "a_v_norm": row(a_v_norm[l]), "a_w_s": a_w_s[l], "a_b_s_t": a_b_s[l].T,
        "a_w_s0": row(jnp.repeat(a_w_s[l][:, 0, 0], CHUNK)), "a_b_s0": row(jnp.repeat(a_b_s[l][:, 0], CHUNK)),
        "q_norm": row(q_norm[l]), "kv_norm": row(kv_norm[l]),
        "w_uq": w_uq_ext.astype(BF16), "w_k": w_k.astype(BF16), "w_vt": w_vt.astype(BF16),
        "w_uk_abs": w_uk_abs.astype(BF16), "w_uv_abs": w_uv_abs.astype(BF16),
        "conv_w": conv_w[l], "conv_b": row(conv_b[l]), "conv_ln_g": row(conv_ln_g[l]),
        "conv_ln_b": row(conv_ln_b[l]),
        "w_branch": w_branch[l].astype(BF16), "w_out": w_out[l].astype(BF16),
        "w_mq": w_mq[l].astype(BF16), "w_mo": w_mo[l].astype(BF16),
        "w_ff1": w_ff1[l].astype(BF16), "w_ff2": w_ff2[l].astype(BF16),
    }


def _row_tile(rows, want):
    t = min(rows, want)
    while rows % t:
        t //= 2
    return t


def kernel(x_prompt, x_sample, mem_prompt, cache_latent, cache_conv, cache_mem_k, cache_mem_v, page_table,
           norm_gains, mem_norm, w_in, a_v_norm, a_w_s, a_b_s, q_norm, w_uq, kv_norm, w_ukv, conv_w, conv_b,
           conv_ln_g, conv_ln_b, w_branch, w_out, w_mq, w_mk, w_mv, w_mo, w_ff1, w_ff2):
    depth = w_in.shape[0]
    n_seq, seq, _ = x_prompt.shape
    n_s, t_s, _ = x_sample.shape
    assert t_s == 1 and seq % CHUNK == 0 and seq >= CONV_HALO
    n_pages = page_table.shape[1]
    past = n_pages * PAGE

    t_attn = _row_tile(seq // 2, 512)
    tm_mix = t_attn
    tm_row = _row_tile(seq, 512)
    tm_s = n_s
    mem_group = _row_tile(n_s, 8)

    cos_p, sin_p = _rope_tables(jnp.arange(seq, dtype=jnp.int32))
    cos_s, sin_s = _rope_tables(jnp.full((n_s,), past, jnp.int32))

    xp = x_prompt.reshape(n_seq * seq, D_MODEL)
    xs = x_sample.reshape(n_s, D_MODEL)
    mem = mem_prompt.reshape(n_seq * N_MEM, D_MODEL)
    cache_k = jnp.transpose(cache_mem_k, (0, 1, 3, 4, 2)).reshape(depth, n_s, MEM_WIDTH, N_MEM)
    cache_v = jnp.transpose(cache_mem_v, (0, 1, 3, 4, 2)).reshape(depth, n_s, MEM_WIDTH, N_MEM)
    cache_lat = jnp.transpose(cache_latent, (0, 1, 3, 2))

    lat_p_l, lat_s_l, conv_p_l, conv_s_l, v_s_l, mk_p_l, mv_p_l = [], [], [], [], [], [], []
    for l in range(depth):
        lw = _layer_weights(l, norm_gains, w_in, a_v_norm, a_w_s, a_b_s, q_norm, w_uq, kv_norm, w_ukv,
                            conv_w, conv_b, conv_ln_g, conv_ln_b, w_branch, w_out, w_mq, w_mo, w_ff1, w_ff2)

        mk, mv = _memory_kv(mem, mem_norm[l].reshape(1, -1), w_mk[l].astype(BF16), w_mv[l].astype(BF16))
        out_a, q, lat, k, vt, out_c, gates, tail = _mix_prompt(xp, cos_p, sin_p, lw, tm=tm_mix)
        out_b = _attn_prompt(q, k, vt, seq=seq, t=t_attn)
        x1, qm = _merge(xp, out_a, out_b, out_c, gates, lw, tm=tm_row)
        o_mem = _mem_attn_prompt(qm, mk.reshape(n_seq, N_MEM, MEM_WIDTH), mv.reshape(n_seq, N_MEM, MEM_WIDTH),
                                 seq=seq, tm=tm_row)
        xp = _mlp(x1, o_mem, lw, tm=tm_row)
        lat_p_l.append(jnp.transpose(lat, (0, 2, 1)))
        conv_p_l.append(tail[:, CONV_HALO - (CONV_W - 1):, :])
        mk_p_l.append(mk.reshape(n_seq, N_MEM, MEM_HEADS, MEM_HEAD_DIM))
        mv_p_l.append(mv.reshape(n_seq, N_MEM, MEM_HEADS, MEM_HEAD_DIM))

        conv_t = jnp.transpose(cache_conv[l], (1, 0, 2))
        out_a, q, lat, out_c, gates, v_n, glu = _mix_sample(xs, cos_s, sin_s, lw, conv_t)
        q_lat, q_rope = _q_latent(q, lw["w_uk_abs"])
        o_lat = _attn_sample(page_table, jnp.transpose(q_lat, (1, 0, 2)), jnp.transpose(q_rope, (1, 0, 2)),
                             lat.reshape(n_s, 1, LATENT), cache_lat, layer=l)
        out_b = _v_up(jnp.transpose(o_lat, (1, 0, 2)), lw["w_uv_abs"])
        x1, qm = _merge(xs, out_a, out_b, out_c, gates, lw, tm=tm_s)
        o_mem = _mem_attn_sample(qm, cache_k, cache_v, layer=l, group=mem_group)
        xs = _mlp(x1, o_mem, lw, tm=tm_s)
        lat_s_l.append(lat.reshape(n_s, 1, LATENT))
        conv_s_l.append(jnp.concatenate([cache_conv[l][:, 1:, :], glu[:, None, :]], axis=1))
        v_s_l.append(v_n.reshape(n_s, 1, BRANCH))

    return (xp.reshape(n_seq, seq, D_MODEL), xs.reshape(n_s, 1, D_MODEL),
            jnp.stack(lat_p_l), jnp.stack(lat_s_l), jnp.stack(conv_p_l), jnp.stack(conv_s_l),
            jnp.stack(v_s_l), jnp.stack(mk_p_l), jnp.stack(mv_p_l))
```

```python
import functools

import jax
import jax.numpy as jnp
import numpy as np
from jax import lax
from jax.experimental import pallas as pl
from jax.experimental.pallas import tpu as pltpu

F32 = jnp.float32
BF16 = jnp.bfloat16

D_MODEL = 1024
BRANCH = 512
N_BRANCH = 3
CHUNK = 128
A_GROUPS = 4
MLA_HEADS = 8
Q_LORA = 256
KV_LORA = 256
NOPE = 64
ROPE = 32
V_DIM = 64
LATENT = KV_LORA + ROPE
ROPE_THETA = 10000.0
CONV_W = 31
N_MEM = 256
MEM_HEADS = 4
MEM_HEAD_DIM = 64
MEM_WIDTH = MEM_HEADS * MEM_HEAD_DIM
D_FF = 4 * D_MODEL
EPS = 1e-6
PAGE = 128

HEAD_BLOCK = 128
V_AUG = V_DIM + 16
QK_WIDTH = MLA_HEADS * HEAD_BLOCK
Q_SCALE = (NOPE + ROPE) ** -0.5
MEM_SCALE = MEM_HEAD_DIM ** -0.5
LOG2_E = float(np.log2(np.e))

COL_A = 0
COL_B = COL_A + 2 * BRANCH
COL_C = COL_B + Q_LORA + KV_LORA + 2 * HEAD_BLOCK
COL_G = COL_C + 2 * BRANCH
IN_EXT = COL_G + N_BRANCH * D_MODEL

CONV_HALO = 32
VMEM_LIMIT = 56 * 1024 * 1024


def _params(*sem):
    return pltpu.CompilerParams(dimension_semantics=sem, vmem_limit_bytes=VMEM_LIMIT)


def _const_spec(shape):
    n = len(shape)
    return pl.BlockSpec(shape, lambda *_: (0,) * n, pipeline_mode=pl.Buffered(1))


def _dot(a, b):
    return jnp.dot(a, b, preferred_element_type=F32)


def _dot_nt(a, b):
    return lax.dot_general(a, b, (((1,), (1,)), ((), ())), preferred_element_type=F32)


def _rms(x, g):
    return x * lax.rsqrt(jnp.mean(x * x, axis=-1, keepdims=True) + EPS) * g


def _gelu(x):
    c = np.sqrt(2.0 / np.pi).astype(np.float32)
    return x * (0.5 * (1.0 + jnp.tanh(c * (x + 0.044715 * (x * x * x)))))


def _sigmoid(x):
    return 1.0 / (1.0 + jnp.exp(-x))


def _silu_layer_norm(x, g, b):
    mu = jnp.mean(x, axis=-1, keepdims=True)
    xc = x - mu
    var = jnp.mean(xc * xc, axis=-1, keepdims=True)
    y = xc * lax.rsqrt(var + EPS) * g + b
    return y * _sigmoid(y)


def _mix_front(x_ref, g0_ref, w_in_ref, cos_ref, sin_ref, avn_ref, qn_ref, kvn_ref, w_uq_ref,
               q_ref, lat_ref, gates_ref, q_scale, lat_transposed):
    h = _rms(x_ref[...], g0_ref[...]).astype(BF16)

    za = _dot_nt(h, w_in_ref[COL_A:COL_B, :])
    u = _gelu(za[:, :BRANCH])
    v_n = _rms(_gelu(za[:, BRANCH:]), avn_ref[...])

    zb = _dot_nt(h, w_in_ref[COL_B:COL_C, :])
    c_q = _rms(zb[:, :Q_LORA], qn_ref[...])
    c_kv = _rms(zb[:, Q_LORA:Q_LORA + KV_LORA], kvn_ref[...])
    cos = cos_ref[...]
    sin = sin_ref[...]
    o = Q_LORA + KV_LORA
    k_rope = zb[:, o:o + HEAD_BLOCK] * cos + zb[:, o + HEAD_BLOCK:o + 2 * HEAD_BLOCK] * sin

    qq = _dot(c_q.astype(BF16), w_uq_ref[...]) * q_scale
    for hd in range(MLA_HEADS):
        lo = hd * HEAD_BLOCK
        q_h = qq[:, lo:lo + HEAD_BLOCK] * cos + qq[:, QK_WIDTH + lo:QK_WIDTH + lo + HEAD_BLOCK] * sin
        q_ref[:, lo:lo + HEAD_BLOCK] = q_h.astype(q_ref.dtype)

    if lat_transposed:
        lat_ref[:KV_LORA, :] = c_kv.T
        lat_ref[KV_LORA:, :] = k_rope.T[:ROPE, :]
    else:
        lat_ref[:, :KV_LORA] = c_kv
        lat_ref[:, KV_LORA:] = k_rope[:, :ROPE]

    zc = _dot_nt(h, w_in_ref[COL_C:COL_G, :])
    glu = zc[:, :BRANCH] * _sigmoid(zc[:, BRANCH:])

    for k in range(N_BRANCH):
        lo = COL_G + k * D_MODEL
        gates_ref[:, k * D_MODEL:(k + 1) * D_MODEL] = _sigmoid(_dot_nt(h, w_in_ref[lo:lo + D_MODEL, :]))
    return u, v_n, c_kv, k_rope, glu


def _mix_prompt_kernel(x_ref, cos_ref, sin_ref, g0_ref, w_in_ref, avn_ref, ws_ref, bs_ref, qn_ref,
                       kvn_ref, w_uq_ref, w_k_ref, w_vt_ref, cw_ref, cb_ref, lng_ref, lnb_ref,
                       outa_ref, q_ref, lat_ref, k_ref, vt_ref, outc_ref, gates_ref, tail_ref,
                       xp_ref, sh_ref, *, tm, tiles_per_seq):
    u, v_n, c_kv, k_rope, glu = _mix_front(x_ref, g0_ref, w_in_ref, cos_ref, sin_ref, avn_ref,
                                           qn_ref, kvn_ref, w_uq_ref, q_ref, lat_ref, gates_ref,
                                           Q_SCALE * LOG2_E, True)

    row = lax.broadcasted_iota(jnp.int32, (CHUNK, CHUNK), 0)
    col = lax.broadcasted_iota(jnp.int32, (CHUNK, CHUNK), 1)
    causal = col <= row
    for g in range(A_GROUPS):
        w_g = jnp.where(causal, ws_ref[g], 0.0).astype(BF16)
        b_g = bs_ref[:, g:g + 1]
        for c in range(tm // CHUNK):
            rs = slice(c * CHUNK, (c + 1) * CHUNK)
            cs = slice(g * CHUNK, (g + 1) * CHUNK)
            mixed = _dot(w_g, v_n[rs, cs].astype(BF16)) + b_g
            outa_ref[rs, cs] = (u[rs, cs] * mixed).astype(outa_ref.dtype)

    c_kv16 = c_kv.astype(BF16)
    k_nope = _dot(c_kv16, w_k_ref[...])
    for hd in range(MLA_HEADS):
        lo = hd * HEAD_BLOCK
        k_ref[:, lo:lo + HEAD_BLOCK] = (k_nope[:, lo:lo + HEAD_BLOCK] + k_rope).astype(k_ref.dtype)
    v_t = _dot_nt(w_vt_ref[...], c_kv16).astype(vt_ref.dtype)
    ones_rows = jnp.ones((V_AUG - V_DIM, tm), vt_ref.dtype)
    for hd in range(MLA_HEADS):
        vt_ref[hd * V_AUG:hd * V_AUG + V_DIM, :] = v_t[hd * V_DIM:(hd + 1) * V_DIM, :]
        vt_ref[hd * V_AUG + V_DIM:(hd + 1) * V_AUG, :] = ones_rows

    first = (pl.program_id(0) % tiles_per_seq) == 0

    @pl.when(first)
    def _():
        xp_ref[:CONV_HALO, :] = jnp.zeros((CONV_HALO, BRANCH), F32)

    @pl.when(jnp.logical_not(first))
    def _():
        xp_ref[:CONV_HALO, :] = xp_ref[tm:tm + CONV_HALO, :]

    xp_ref[CONV_HALO:, :] = glu
    acc = jnp.zeros((tm, BRANCH), F32) + cb_ref[...]
    off = CONV_HALO - (CONV_W - 1)
    for phase in range(8):
        taps = [j for j in range(CONV_W) if (off + j) % 8 == phase]
        if not taps:
            continue
        base = min(off + j for j in taps)
        span = max(off + j for j in taps) - base + tm
        if phase == 0:
            src_ref, src_base = xp_ref, base
        else:
            sh_ref[:span, :] = xp_ref[base:base + span, :]
            src_ref, src_base = sh_ref, 0
        for j in taps:
            lo = src_base + off + j - base
            acc = acc + cw_ref[j:j + 1, :] * src_ref[lo:lo + tm, :]
    outc_ref[...] = _silu_layer_norm(acc, lng_ref[...], lnb_ref[...]).astype(outc_ref.dtype)
    tail_ref[...] = xp_ref[tm:tm + CONV_HALO, :]


def _mix_sample_kernel(x_ref, cos_ref, sin_ref, g0_ref, w_in_ref, avn_ref, ws0_ref, bs0_ref, qn_ref,
                       kvn_ref, w_uq_ref, cw_ref, cb_ref, lng_ref, lnb_ref, conv_ref,
                       outa_ref, q_ref, lat_ref, outc_ref, gates_ref, vn_ref, glu_ref):
    u, v_n, _, _, glu = _mix_front(x_ref, g0_ref, w_in_ref, cos_ref, sin_ref, avn_ref,
                                   qn_ref, kvn_ref, w_uq_ref, q_ref, lat_ref, gates_ref, Q_SCALE, False)
    vn_ref[...] = v_n
    glu_ref[...] = glu
    outa_ref[...] = (u * (ws0_ref[...] * v_n + bs0_ref[...])).astype(outa_ref.dtype)
    acc = cb_ref[...] + cw_ref[CONV_W - 1:CONV_W, :] * glu
    for j in range(CONV_W - 1):
        acc = acc + cw_ref[j:j + 1, :] * conv_ref[j]
    outc_ref[...] = _silu_layer_norm(acc, lng_ref[...], lnb_ref[...]).astype(outc_ref.dtype)


def _mix_prompt(x, cos, sin, lw, *, tm):
    rows = x.shape[0]
    seq = cos.shape[0]
    tiles_per_seq = seq // tm
    n_seq = rows // seq
    grid = (rows // tm,)
    row_spec = lambda w: pl.BlockSpec((tm, w), lambda i: (i, 0))
    pos_spec = pl.BlockSpec((tm, HEAD_BLOCK), lambda i: (i % tiles_per_seq, 0))
    in_specs = [
        row_spec(D_MODEL), pos_spec, pos_spec,
        _const_spec((1, D_MODEL)), _const_spec((IN_EXT, D_MODEL)), _const_spec((1, BRANCH)),
        _const_spec((A_GROUPS, CHUNK, CHUNK)), _const_spec((CHUNK, A_GROUPS)),
        _const_spec((1, Q_LORA)), _const_spec((1, KV_LORA)),
        _const_spec((Q_LORA, 2 * QK_WIDTH)), _const_spec((KV_LORA, QK_WIDTH)),
        _const_spec((BRANCH, KV_LORA)),
        _const_spec((CONV_W, BRANCH)), _const_spec((1, BRANCH)), _const_spec((1, BRANCH)),
        _const_spec((1, BRANCH)),
    ]
    out_shape = (
        jax.ShapeDtypeStruct((rows, BRANCH), BF16),
        jax.ShapeDtypeStruct((rows, QK_WIDTH), BF16),
        jax.ShapeDtypeStruct((n_seq, LATENT, seq), F32),
        jax.ShapeDtypeStruct((rows, QK_WIDTH), BF16),
        jax.ShapeDtypeStruct((n_seq, tiles_per_seq, MLA_HEADS * V_AUG, tm), BF16),
        jax.ShapeDtypeStruct((rows, BRANCH), BF16),
        jax.ShapeDtypeStruct((rows, N_BRANCH * D_MODEL), F32),
        jax.ShapeDtypeStruct((n_seq, CONV_HALO, BRANCH), F32),
    )
    out_specs = (
        row_spec(BRANCH), row_spec(QK_WIDTH),
        pl.BlockSpec((None, LATENT, tm), lambda i: (i // tiles_per_seq, 0, i % tiles_per_seq)),
        row_spec(QK_WIDTH),
        pl.BlockSpec((None, None, MLA_HEADS * V_AUG, tm),
                     lambda i: (i // tiles_per_seq, i % tiles_per_seq, 0, 0)),
        row_spec(BRANCH), row_spec(N_BRANCH * D_MODEL),
        pl.BlockSpec((None, CONV_HALO, BRANCH), lambda i: (i // tiles_per_seq, 0, 0)),
    )
    return pl.pallas_call(
        functools.partial(_mix_prompt_kernel, tm=tm, tiles_per_seq=tiles_per_seq),
        grid=grid, in_specs=in_specs, out_specs=out_specs, out_shape=out_shape,
        scratch_shapes=[pltpu.VMEM((tm + CONV_HALO, BRANCH), F32)] * 2,
        compiler_params=_params("arbitrary"), name="mix_prompt",
    )(x, cos, sin, lw["g0"], lw["w_in"], lw["a_v_norm"], lw["a_w_s"], lw["a_b_s_t"], lw["q_norm"],
      lw["kv_norm"], lw["w_uq"], lw["w_k"], lw["w_vt"], lw["conv_w"], lw["conv_b"], lw["conv_ln_g"],
      lw["conv_ln_b"])


def _mix_sample(x, cos, sin, lw, conv_t):
    rows = x.shape[0]
    full = lambda shape: pl.BlockSpec(shape, lambda i: (0,) * len(shape))
    in_specs = [
        full((rows, D_MODEL)), full((rows, HEAD_BLOCK)), full((rows, HEAD_BLOCK)),
        full((1, D_MODEL)), _const_spec((IN_EXT, D_MODEL)), full((1, BRANCH)),
        full((1, BRANCH)), full((1, BRANCH)), full((1, Q_LORA)), full((1, KV_LORA)),
        full((Q_LORA, 2 * QK_WIDTH)), full((CONV_W, BRANCH)), full((1, BRANCH)), full((1, BRANCH)),
        full((1, BRANCH)), full((CONV_W - 1, rows, BRANCH)),
    ]
    out_shape = (
        jax.ShapeDtypeStruct((rows, BRANCH), BF16),
        jax.ShapeDtypeStruct((rows, QK_WIDTH), F32),
        jax.ShapeDtypeStruct((rows, LATENT), F32),
        jax.ShapeDtypeStruct((rows, BRANCH), BF16),
        jax.ShapeDtypeStruct((rows, N_BRANCH * D_MODEL), F32),
        jax.ShapeDtypeStruct((rows, BRANCH), F32),
        jax.ShapeDtypeStruct((rows, BRANCH), F32),
    )
    out_specs = tuple(full(s.shape) for s in out_shape)
    return pl.pallas_call(
        _mix_sample_kernel, grid=(1,), in_specs=in_specs, out_specs=out_specs, out_shape=out_shape,
        compiler_params=_params("arbitrary"), name="mix_sample",
    )(x, cos, sin, lw["g0"], lw["w_in"], lw["a_v_norm"], lw["a_w_s0"], lw["a_b_s0"], lw["q_norm"],
      lw["kv_norm"], lw["w_uq"], lw["conv_w"], lw["conv_b"], lw["conv_ln_g"], lw["conv_ln_b"], conv_t)


def _attn_prompt_kernel(qi_ref, kj_ref, q_ref, k_ref, vt_ref, o_ref, m_ref, acc_ref, *, t):
    step = pl.program_id(1)
    qi = qi_ref[step]
    kj = kj_ref[step]

    @pl.when(kj == 0)
    def _():
        m_ref[...] = jnp.full(m_ref.shape, -jnp.inf, F32)
        acc_ref[...] = jnp.zeros(acc_ref.shape, F32)

    def update(sub, masked):
        if masked:
            key = lax.broadcasted_iota(jnp.int32, (t, t), 0)
            qry = lax.broadcasted_iota(jnp.int32, (t, t), 1)
            keep = key <= qry
        rows = slice(sub * t, (sub + 1) * t)

        def scores(hd):
            lo = hd * HEAD_BLOCK
            return _dot_nt(k_ref[rows, lo:lo + HEAD_BLOCK], q_ref[:, lo:lo + HEAD_BLOCK])

        ahead = 2
        pending = [scores(hd) for hd in range(ahead)]
        for hd in range(MLA_HEADS):
            vs = slice(hd * V_DIM, (hd + 1) * V_DIM)
            s = pending.pop(0)
            if hd + ahead < MLA_HEADS:
                pending.append(scores(hd + ahead))
            if masked:
                s = jnp.where(keep, s, -jnp.inf)
            m_old = m_ref[hd]
            m_new = jnp.maximum(m_old, jnp.max(s, axis=0, keepdims=True))
            alpha = jnp.exp2(m_old - m_new)
            p = jnp.exp2(s - m_new)
            m_ref[hd] = m_new
            v_aug = vt_ref[sub, hd * V_AUG:(hd + 1) * V_AUG, :]
            acc_ref[hd] = acc_ref[hd] * alpha + _dot(v_aug, p.astype(BF16))

    def finalize():
        outs = []
        for hd in range(MLA_HEADS):
            acc = acc_ref[hd]
            outs.append(acc[:V_DIM, :] / acc[V_DIM:V_DIM + 1, :])
        o_ref[...] = jnp.concatenate(outs, axis=0).T.astype(o_ref.dtype)

    last_pair = 2 * kj + 1 >= qi

    @pl.when(jnp.logical_not(last_pair))
    def _():
        update(0, False)
        update(1, False)

    @pl.when(jnp.logical_and(last_pair, qi % 2 == 0))
    def _():
        update(0, True)
        finalize()

    @pl.when(jnp.logical_and(last_pair, qi % 2 == 1))
    def _():
        update(0, False)
        update(1, True)
        finalize()


def _attn_prompt(q, k, vt, *, seq, t):
    rows = q.shape[0]
    n_seq = rows // seq
    nq = seq // t
    n_pairs = nq // 2
    assert nq % 2 == 0 and vt.shape == (n_seq, nq, MLA_HEADS * V_AUG, t), (nq, vt.shape)
    steps = [(i, j) for i in range(nq) for j in range(i // 2 + 1)]
    qi_tab = np.array([i for i, _ in steps], np.int32)
    kj_tab = np.array([j for _, j in steps], np.int32)
    grid_spec = pltpu.PrefetchScalarGridSpec(
        num_scalar_prefetch=2,
        grid=(n_seq, len(steps)),
        in_specs=[
            pl.BlockSpec((t, QK_WIDTH), lambda b, s, qi, kj: (b * nq + qi[s], 0)),
            pl.BlockSpec((2 * t, QK_WIDTH), lambda b, s, qi, kj: (b * n_pairs + kj[s], 0)),
            pl.BlockSpec((None, 2, MLA_HEADS * V_AUG, t), lambda b, s, qi, kj: (b, kj[s], 0, 0)),
        ],
        out_specs=pl.BlockSpec((t, BRANCH), lambda b, s, qi, kj: (b * nq + qi[s], 0)),
        scratch_shapes=[
            pltpu.VMEM((MLA_HEADS, 1, t), F32),
            pltpu.VMEM((MLA_HEADS, V_AUG, t), F32),
        ],
    )
    return pl.pallas_call(
        functools.partial(_attn_prompt_kernel, t=t),
        grid_spec=grid_spec, out_shape=jax.ShapeDtypeStruct((rows, BRANCH), BF16),
        compiler_params=_params("arbitrary", "arbitrary"), name="attn_prompt",
    )(jnp.asarray(qi_tab), jnp.asarray(kj_tab), q, k, vt)


def _q_latent_kernel(q_ref, w_uk_ref, qlat_ref, qrope_ref):
    lane = lax.broadcasted_iota(jnp.int32, (q_ref.shape[0], HEAD_BLOCK), 1)
    for hd in range(MLA_HEADS):
        q_h = q_ref[:, hd * HEAD_BLOCK:(hd + 1) * HEAD_BLOCK]
        qlat_ref[hd] = _dot(q_h.astype(BF16), w_uk_ref[hd])
        qrope_ref[hd] = jnp.where(lane < ROPE, q_h, 0.0)


def _q_latent(q, w_uk):
    rows = q.shape[0]
    full = lambda shape: pl.BlockSpec(shape, lambda i: (0,) * len(shape))
    return pl.pallas_call(
        _q_latent_kernel, grid=(1,),
        in_specs=[full((rows, QK_WIDTH)), full((MLA_HEADS, HEAD_BLOCK, KV_LORA))],
        out_specs=(full((MLA_HEADS, rows, KV_LORA)), full((MLA_HEADS, rows, HEAD_BLOCK))),
        out_shape=(jax.ShapeDtypeStruct((MLA_HEADS, rows, KV_LORA), F32),
                   jax.ShapeDtypeStruct((MLA_HEADS, rows, HEAD_BLOCK), F32)),
        compiler_params=_params("arbitrary"), name="q_latent",
    )(q, w_uk)


def _attn_sample_kernel(pt_ref, qlat_ref, qrope_ref, latn_ref, cache_ref, o_ref, buf_ref, sem_ref, *,
                        layer, n_pages, key_chunk):
    b = pl.program_id(0)
    nb = pl.num_programs(0)
    past = n_pages * PAGE

    def page_copy(sample, page, slot):
        return pltpu.make_async_copy(
            cache_ref.at[layer, pt_ref[sample, page]],
            buf_ref.at[slot, :, pl.ds(pl.multiple_of(page * PAGE, PAGE), PAGE)],
            sem_ref.at[slot])

    def start_all(sample, slot):
        def body(page, carry):
            page_copy(sample, page, slot).start()
            return carry
        lax.fori_loop(0, n_pages, body, 0, unroll=8 if n_pages % 8 == 0 else 1)

    def wait_all(sample, slot):
        def body(page, carry):
            page_copy(sample, page, slot).wait()
            return carry
        lax.fori_loop(0, n_pages, body, 0, unroll=8 if n_pages % 8 == 0 else 1)

    slot = b % 2

    @pl.when(b == 0)
    def _():
        start_all(0, 0)

    @pl.when(b + 1 < nb)
    def _():
        start_all(b + 1, 1 - slot)

    wait_all(b, slot)

    q_lat = qlat_ref[...].astype(BF16)
    q_rope = qrope_ref[:, :ROPE].astype(BF16)
    lat_new = latn_ref[...].astype(BF16).astype(F32)
    s_new = (jnp.sum(q_lat.astype(F32) * lat_new[:, :KV_LORA], axis=-1, keepdims=True)
             + jnp.sum(q_rope.astype(F32) * lat_new[:, KV_LORA:], axis=-1, keepdims=True))

    n_chunks = past // key_chunk
    scores = []
    for c in range(n_chunks):
        cols = buf_ref[slot, :, c * key_chunk:(c + 1) * key_chunk].astype(BF16)
        scores.append(_dot(q_lat, cols[:KV_LORA, :]) + _dot(q_rope, cols[KV_LORA:, :]))
    m = s_new
    for s in scores:
        m = jnp.maximum(m, jnp.max(s, axis=-1, keepdims=True))
    exps = [jnp.exp(s - m) for s in scores]
    e_new = jnp.exp(s_new - m)
    denom = e_new
    for e in exps:
        denom = denom + jnp.sum(e, axis=-1, keepdims=True)
    p_new = (e_new / denom).astype(BF16).astype(F32)
    out = p_new * lat_new[:, :KV_LORA]
    for c in range(n_chunks):
        p = (exps[c] / denom).astype(BF16)
        cols = buf_ref[slot, :KV_LORA, c * key_chunk:(c + 1) * key_chunk].astype(BF16)
        out = out + _dot_nt(p, cols)
    o_ref[...] = out


def _attn_sample(page_table, q_lat, q_rope, lat_new, cache_latent, *, layer):
    n_s, n_pages = page_table.shape
    past = n_pages * PAGE
    key_chunk = 1024 if past % 1024 == 0 else PAGE
    grid_spec = pltpu.PrefetchScalarGridSpec(
        num_scalar_prefetch=1,
        grid=(n_s,),
        in_specs=[
            pl.BlockSpec((None, MLA_HEADS, KV_LORA), lambda b, pt: (b, 0, 0)),
            pl.BlockSpec((None, MLA_HEADS, HEAD_BLOCK), lambda b, pt: (b, 0, 0)),
            pl.BlockSpec((None, 1, LATENT), lambda b, pt: (b, 0, 0)),
            pl.BlockSpec(memory_space=pl.ANY),
        ],
        out_specs=pl.BlockSpec((None, MLA_HEADS, KV_LORA), lambda b, pt: (b, 0, 0)),
        scratch_shapes=[
            pltpu.VMEM((2, LATENT, past), F32),
            pltpu.SemaphoreType.DMA((2,)),
        ],
    )
    return pl.pallas_call(
        functools.partial(_attn_sample_kernel, layer=layer, n_pages=n_pages, key_chunk=key_chunk),
        grid_spec=grid_spec, out_shape=jax.ShapeDtypeStruct((n_s, MLA_HEADS, KV_LORA), F32),
        compiler_params=_params("arbitrary"), name="attn_sample",
    )(page_table, q_lat, q_rope, lat_new, cache_latent)


def _v_up_kernel(o_ref, w_uv_ref, out_ref):
    acc = _dot(o_ref[0].astype(BF16), w_uv_ref[0])
    for hd in range(1, MLA_HEADS):
        acc = acc + _dot(o_ref[hd].astype(BF16), w_uv_ref[hd])
    out_ref[...] = acc.astype(out_ref.dtype)


def _v_up(o_lat, w_uv):
    rows = o_lat.shape[1]
    full = lambda shape: pl.BlockSpec(shape, lambda i: (0,) * len(shape))
    return pl.pallas_call(
        _v_up_kernel, grid=(1,),
        in_specs=[full((MLA_HEADS, rows, KV_LORA)), full((MLA_HEADS, KV_LORA, BRANCH))],
        out_specs=full((rows, BRANCH)),
        out_shape=jax.ShapeDtypeStruct((rows, BRANCH), BF16),
        compiler_params=_params("arbitrary"), name="v_up",
    )(o_lat, w_uv)


def _merge_math(x_ref, a_ref, b_ref, c_ref, gates_ref, wb_ref, wo_ref, g1_ref, g2_ref, wmq_ref):
    merged = None
    for k, br_ref in enumerate((a_ref, b_ref, c_ref)):
        term = gates_ref[:, k * D_MODEL:(k + 1) * D_MODEL] * _dot(br_ref[...], wb_ref[k])
        merged = term if merged is None else merged + term
    y = _dot(merged.astype(BF16), wo_ref[...])
    x1 = x_ref[...] + _rms(y, g1_ref[...])
    h = _rms(x1, g2_ref[...]).astype(BF16)
    return x1, _dot(h, wmq_ref[...]) * MEM_SCALE


def _merge_kernel(x_ref, a_ref, b_ref, c_ref, gates_ref, wb_ref, wo_ref, g1_ref, g2_ref, wmq_ref,
                  x1_ref, qm_ref):
    x1_ref[...], qm_ref[...] = _merge_math(x_ref, a_ref, b_ref, c_ref, gates_ref, wb_ref, wo_ref,
                                           g1_ref, g2_ref, wmq_ref)


def _merge(x, out_a, out_b, out_c, gates, lw, *, tm):
    rows = x.shape[0]
    row_spec = lambda w: pl.BlockSpec((tm, w), lambda i: (i, 0))
    return pl.pallas_call(
        _merge_kernel, grid=(rows // tm,),
        in_specs=[row_spec(D_MODEL), row_spec(BRANCH), row_spec(BRANCH), row_spec(BRANCH),
                  row_spec(N_BRANCH * D_MODEL), _const_spec((N_BRANCH, BRANCH, D_MODEL)),
                  _const_spec((D_MODEL, D_MODEL)), _const_spec((1, D_MODEL)), _const_spec((1, D_MODEL)),
                  _const_spec((D_MODEL, MEM_WIDTH))],
        out_specs=(row_spec(D_MODEL), row_spec(MEM_WIDTH)),
        out_shape=(jax.ShapeDtypeStruct((rows, D_MODEL), F32),
                   jax.ShapeDtypeStruct((rows, MEM_WIDTH), F32)),
        compiler_params=_params("arbitrary"), name="merge",
    )(x, out_a, out_b, out_c, gates, lw["w_branch"], lw["w_out"], lw["g1"], lw["g2"], lw["w_mq"])


def _memory_kv_kernel(mem_ref, g_ref, wk_ref, wv_ref, k_ref, v_ref):
    m = _rms(mem_ref[...], g_ref[...]).astype(BF16)
    k_ref[...] = _dot(m, wk_ref[...])
    v_ref[...] = _dot(m, wv_ref[...])


def _memory_kv(mem, g, w_mk, w_mv):
    rows = mem.shape[0]
    full = lambda shape: pl.BlockSpec(shape, lambda i: (0,) * len(shape))
    return pl.pallas_call(
        _memory_kv_kernel, grid=(1,),
        in_specs=[full((rows, D_MODEL)), full((1, D_MODEL)), full((D_MODEL, MEM_WIDTH)),
                  full((D_MODEL, MEM_WIDTH))],
        out_specs=(full((rows, MEM_WIDTH)), full((rows, MEM_WIDTH))),
        out_shape=(jax.ShapeDtypeStruct((rows, MEM_WIDTH), F32),) * 2,
        compiler_params=_params("arbitrary"), name="memory_kv",
    )(mem, g, w_mk, w_mv)


def _head_lane_mask(rows):
    lane = lax.broadcasted_iota(jnp.int32, (rows, MEM_WIDTH), 1)
    return [(lane >= hd * MEM_HEAD_DIM) & (lane < (hd + 1) * MEM_HEAD_DIM) for hd in range(MEM_HEADS)]


def _mem_attn_math(qm, k_ref, v_ref):
    k = k_ref[...].astype(BF16)
    v = v_ref[...].astype(BF16)
    masks = _head_lane_mask(qm.shape[0])
    out = jnp.zeros(qm.shape, F32)
    for hd in range(MEM_HEADS):
        s = _dot_nt(jnp.where(masks[hd], qm, 0.0).astype(BF16), k)
        e = jnp.exp(s - jnp.max(s, axis=-1, keepdims=True))
        p = (e / jnp.sum(e, axis=-1, keepdims=True)).astype(BF16)
        out = out + jnp.where(masks[hd], _dot(p, v), 0.0)
    return out.astype(BF16)


def _mem_attn_prompt_kernel(qm_ref, k_ref, v_ref, o_ref):
    o_ref[...] = _mem_attn_math(qm_ref[...], k_ref, v_ref)


def _mem_attn_prompt(qm, mem_k, mem_v, *, seq, tm):
    rows = qm.shape[0]
    tiles_per_seq = seq // tm
    kv_spec = pl.BlockSpec((None, N_MEM, MEM_WIDTH), lambda i: (i // tiles_per_seq, 0, 0))
    return pl.pallas_call(
        _mem_attn_prompt_kernel, grid=(rows // tm,),
        in_specs=[pl.BlockSpec((tm, MEM_WIDTH), lambda i: (i, 0)), kv_spec, kv_spec],
        out_specs=pl.BlockSpec((tm, MEM_WIDTH), lambda i: (i, 0)),
        out_shape=jax.ShapeDtypeStruct((rows, MEM_WIDTH), BF16),
        compiler_params=_params("arbitrary"), name="mem_attn_prompt",
    )(qm, mem_k, mem_v)


def _mem_attn_sample_kernel(qm_ref, kt_ref, vt_ref, o_ref, *, group):
    sub = lax.broadcasted_iota(jnp.int32, (8, MEM_WIDTH), 0)
    lane = lax.broadcasted_iota(jnp.int32, (8, MEM_WIDTH), 1)
    own = (lane >= sub * MEM_HEAD_DIM) & (lane < (sub + 1) * MEM_HEAD_DIM)
    for i in range(group):
        q_rows = jnp.where(own, qm_ref[i:i + 1, :], 0.0).astype(BF16)
        s = _dot(q_rows, kt_ref[i].astype(BF16))
        e = jnp.exp(s - jnp.max(s, axis=-1, keepdims=True))
        p = (e / jnp.sum(e, axis=-1, keepdims=True)).astype(BF16)
        o_all = _dot_nt(p, vt_ref[i].astype(BF16))
        o_ref[i:i + 1, :] = jnp.sum(jnp.where(own, o_all, 0.0), axis=0, keepdims=True).astype(o_ref.dtype)


def _mem_attn_sample(qm, cache_k, cache_v, *, layer, group):
    rows = qm.shape[0]
    kv_spec = pl.BlockSpec((None, group, MEM_WIDTH, N_MEM), lambda i: (layer, i, 0, 0))
    return pl.pallas_call(
        functools.partial(_mem_attn_sample_kernel, group=group), grid=(rows // group,),
        in_specs=[pl.BlockSpec((group, MEM_WIDTH), lambda i: (i, 0)), kv_spec, kv_spec],
        out_specs=pl.BlockSpec((group, MEM_WIDTH), lambda i: (i, 0)),
        out_shape=jax.ShapeDtypeStruct((rows, MEM_WIDTH), F32),
        compiler_params=_params("arbitrary"), name="mem_attn_sample",
    )(qm, cache_k, cache_v)


FF_CHUNK = 1024


def _mlp_math(x1, o, wmo_ref, g3_ref, g4_ref, w1_ref, w2_ref, g5_ref):
    x2 = x1 + _rms(_dot(o.astype(BF16), wmo_ref[...]), g3_ref[...])
    h = _rms(x2, g4_ref[...]).astype(BF16)
    f = None
    for c in range(D_FF // FF_CHUNK):
        cs = slice(c * FF_CHUNK, (c + 1) * FF_CHUNK)
        a = jnp.maximum(_dot(h, w1_ref[:, cs]), 0.0)
        part = _dot((a * a).astype(BF16), w2_ref[cs, :])
        f = part if f is None else f + part
    return x2 + _rms(f, g5_ref[...])


def _mlp_kernel(x1_ref, o_ref, wmo_ref, g3_ref, g4_ref, w1_ref, w2_ref, g5_ref, x3_ref):
    x3_ref[...] = _mlp_math(x1_ref[...], o_ref[...], wmo_ref, g3_ref, g4_ref, w1_ref, w2_ref, g5_ref)


def _mlp(x1, o, lw, *, tm):
    rows = x1.shape[0]
    row_spec = lambda w: pl.BlockSpec((tm, w), lambda i: (i, 0))
    return pl.pallas_call(
        _mlp_kernel, grid=(rows // tm,),
        in_specs=[row_spec(D_MODEL), row_spec(MEM_WIDTH), _const_spec((MEM_WIDTH, D_MODEL)),
                  _const_spec((1, D_MODEL)), _const_spec((1, D_MODEL)), _const_spec((D_MODEL, D_FF)),
                  _const_spec((D_FF, D_MODEL)), _const_spec((1, D_MODEL))],
        out_specs=row_spec(D_MODEL),
        out_shape=jax.ShapeDtypeStruct((rows, D_MODEL), F32),
        compiler_params=_params("arbitrary"), name="mlp",
    )(x1, o, lw["w_mo"], lw["g3"], lw["g4"], lw["w_ff1"], lw["w_ff2"], lw["g5"])


def _rotate_half_cols(w):
    half = ROPE // 2
    return jnp.concatenate([-w[..., half:], w[..., :half]], axis=-1)


def _rope_tables(pos):
    half = ROPE // 2
    inv = 1.0 / (ROPE_THETA ** (jnp.arange(half, dtype=F32) / half))
    ang = pos.astype(F32)[:, None] * inv[None, :]
    cos, sin = jnp.cos(ang), jnp.sin(ang)
    n = pos.shape[0]
    cos_blk = jnp.concatenate([cos, cos, jnp.ones((n, HEAD_BLOCK - ROPE), F32)], axis=1)
    sin_blk = jnp.concatenate([sin, sin, jnp.zeros((n, HEAD_BLOCK - ROPE), F32)], axis=1)
    return cos_blk, sin_blk


def _layer_weights(l, norm_gains, w_in, a_v_norm, a_w_s, a_b_s, q_norm, w_uq, kv_norm, w_ukv, conv_w,
                   conv_b, conv_ln_g, conv_ln_b, w_branch, w_out, w_mq, w_mo, w_ff1, w_ff2):
    row = lambda v: v.reshape(1, -1)
    wit = w_in[l].astype(BF16).T
    o_k = 2 * BRANCH + Q_LORA + KV_LORA
    o_c = o_k + ROPE
    kr_t = wit[o_k:o_c]
    pad_t = jnp.zeros((HEAD_BLOCK - ROPE, D_MODEL), BF16)
    w_in_t = jnp.concatenate([wit[:o_k], kr_t, pad_t, _rotate_half_cols(kr_t.T).T, pad_t, wit[o_c:]], axis=0)

    uq = w_uq[l].reshape(Q_LORA, MLA_HEADS, NOPE + ROPE)
    uq_nope, uq_rope = uq[..., :NOPE], uq[..., NOPE:]
    z = lambda n: jnp.zeros((Q_LORA, MLA_HEADS, n), F32)
    uq_pad = jnp.concatenate([uq_rope, uq_nope, z(HEAD_BLOCK - ROPE - NOPE)], axis=-1)
    uq_rot = jnp.concatenate([_rotate_half_cols(uq_rope), z(HEAD_BLOCK - ROPE)], axis=-1)
    w_uq_ext = jnp.concatenate([uq_pad.reshape(Q_LORA, QK_WIDTH), uq_rot.reshape(Q_LORA, QK_WIDTH)], axis=1)

    ukv = w_ukv[l].reshape(KV_LORA, MLA_HEADS, NOPE + V_DIM)
    uk, uv = ukv[..., :NOPE], ukv[..., NOPE:]
    zk = lambda n: jnp.zeros((KV_LORA, MLA_HEADS, n), F32)
    kn_pad = jnp.concatenate([zk(ROPE), uk, zk(HEAD_BLOCK - ROPE - NOPE)], axis=-1)
    w_k = kn_pad.reshape(KV_LORA, QK_WIDTH)
    w_vt = uv.reshape(KV_LORA, BRANCH).T
    uk_t = jnp.transpose(uk, (1, 2, 0))
    w_uk_abs = jnp.concatenate([jnp.zeros((MLA_HEADS, ROPE, KV_LORA), F32), uk_t,
                                jnp.zeros((MLA_HEADS, HEAD_BLOCK - ROPE - NOPE, KV_LORA), F32)], axis=1)
    eye = jnp.eye(MLA_HEADS, dtype=F32)
    w_uv_abs = (jnp.transpose(uv, (1, 0, 2))[:, :, None, :] * eye[:, None, :, None]).reshape(
        MLA_HEADS, KV_LORA, BRANCH)

    g = norm_gains[l]
    return {
        "g0": row(g[0]), "g1": row(g[1]), "g2": row(g[2]), "g3": row(g[3]), "g4": row(g[4]), "g5": row(g[5]),
        "w_in": w_in_t,
        "a_v_norm": row(a_v_norm[l]), "a_w_s": a_w_s[l], "a_b_s_t": a_b_s[l].T,
        "a_w_s0": row(jnp.repeat(a_w_s[l][:, 0, 0], CHUNK)), "a_b_s0": row(jnp.repeat(a_b_s[l][:, 0], CHUNK)),
        "q_norm": row(q_norm[l]), "kv_norm": row(kv_norm[l]),
        "w_uq": w_uq_ext.astype(BF16), "w_k": w_k.astype(BF16), "w_vt": w_vt.astype(BF16),
        "w_uk_abs": w_uk_abs.astype(BF16), "w_uv_abs": w_uv_abs.astype(BF16),
        "conv_w": conv_w[l], "conv_b": row(conv_b[l]), "conv_ln_g": row(conv_ln_g[l]),
        "conv_ln_b": row(conv_ln_b[l]),
        "w_branch": w_branch[l].astype(BF16), "w_out": w_out[l].astype(BF16),
        "w_mq": w_mq[l].astype(BF16), "w_mo": w_mo[l].astype(BF16),
        "w_ff1": w_ff1[l].astype(BF16), "w_ff2": w_ff2[l].astype(BF16),
    }


def _row_tile(rows, want):
    t = min(rows, want)
    while rows % t:
        t //= 2
    return t


def kernel(x_prompt, x_sample, mem_prompt, cache_latent, cache_conv, cache_mem_k, cache_mem_v, page_table,
           norm_gains, mem_norm, w_in, a_v_norm, a_w_s, a_b_s, q_norm, w_uq, kv_norm, w_ukv, conv_w, conv_b,
           conv_ln_g, conv_ln_b, w_branch, w_out, w_mq, w_mk, w_mv, w_mo, w_ff1, w_ff2):
    depth = w_in.shape[0]
    n_seq, seq, _ = x_prompt.shape
    n_s, t_s, _ = x_sample.shape
    assert t_s == 1 and seq % CHUNK == 0 and seq >= CONV_HALO
    n_pages = page_table.shape[1]
    past = n_pages * PAGE

    t_attn = _row_tile(seq // 2, 512)
    tm_mix = t_attn
    tm_row = _row_tile(seq, 512)
    tm_s = n_s
    mem_group = _row_tile(n_s, 8)

    cos_p, sin_p = _rope_tables(jnp.arange(seq, dtype=jnp.int32))
    cos_s, sin_s = _rope_tables(jnp.full((n_s,), past, jnp.int32))

    xp = x_prompt.reshape(n_seq * seq, D_MODEL)
    xs = x_sample.reshape(n_s, D_MODEL)
    mem = mem_prompt.reshape(n_seq * N_MEM, D_MODEL)
    cache_k = jnp.transpose(cache_mem_k, (0, 1, 3, 4, 2)).reshape(depth, n_s, MEM_WIDTH, N_MEM)
    cache_v = jnp.transpose(cache_mem_v, (0, 1, 3, 4, 2)).reshape(depth, n_s, MEM_WIDTH, N_MEM)
    cache_lat = jnp.transpose(cache_latent, (0, 1, 3, 2))

    lat_p_l, lat_s_l, conv_p_l, conv_s_l, v_s_l, mk_p_l, mv_p_l = [], [], [], [], [], [], []
    for l in range(depth):
        lw = _layer_weights(l, norm_gains, w_in, a_v_norm, a_w_s, a_b_s, q_norm, w_uq, kv_norm, w_ukv,
                            conv_w, conv_b, conv_ln_g, conv_ln_b, w_branch, w_out, w_mq, w_mo, w_ff1, w_ff2)

        mk, mv = _memory_kv(mem, mem_norm[l].reshape(1, -1), w_mk[l].astype(BF16), w_mv[l].astype(BF16))
        out_a, q, lat, k, vt, out_c, gates, tail = _mix_prompt(xp, cos_p, sin_p, lw, tm=tm_mix)
        out_b = _attn_prompt(q, k, vt, seq=seq, t=t_attn)
        x1, qm = _merge(xp, out_a, out_b, out_c, gates, lw, tm=tm_row)
        o_mem = _mem_attn_prompt(qm, mk.reshape(n_seq, N_MEM, MEM_WIDTH), mv.reshape(n_seq, N_MEM, MEM_WIDTH),
                                 seq=seq, tm=tm_row)
        xp = _mlp(x1, o_mem, lw, tm=tm_row)
        lat_p_l.append(jnp.transpose(lat, (0, 2, 1)))
        conv_p_l.append(tail[:, CONV_HALO - (CONV_W - 1):, :])
        mk_p_l.append(mk.reshape(n_seq, N_MEM, MEM_HEADS, MEM_HEAD_DIM))
        mv_p_l.append(mv.reshape(n_seq, N_MEM, MEM_HEADS, MEM_HEAD_DIM))

        conv_t = jnp.transpose(cache_conv[l], (1, 0, 2))
        out_a, q, lat, out_c, gates, v_n, glu = _mix_sample(xs, cos_s, sin_s, lw, conv_t)
        q_lat, q_rope = _q_latent(q, lw["w_uk_abs"])
        o_lat = _attn_sample(page_table, jnp.transpose(q_lat, (1, 0, 2)), jnp.transpose(q_rope, (1, 0, 2)),
                             lat.reshape(n_s, 1, LATENT), cache_lat, layer=l)
        out_b = _v_up(jnp.transpose(o_lat, (1, 0, 2)), lw["w_uv_abs"])
        x1, qm = _merge(xs, out_a, out_b, out_c, gates, lw, tm=tm_s)
        o_mem = _mem_attn_sample(qm, cache_k, cache_v, layer=l, group=mem_group)
        xs = _mlp(x1, o_mem, lw, tm=tm_s)
        lat_s_l.append(lat.reshape(n_s, 1, LATENT))
        conv_s_l.append(jnp.concatenate([cache_conv[l][:, 1:, :], glu[:, None, :]], axis=1))
        v_s_l.append(v_n.reshape(n_s, 1, BRANCH))

    return (xp.reshape(n_seq, seq, D_MODEL), xs.reshape(n_s, 1, D_MODEL),
            jnp.stack(lat_p_l), jnp.stack(lat_s_l), jnp.stack(conv_p_l), jnp.stack(conv_s_l),
            jnp.stack(v_s_l), jnp.stack(mk_p_l), jnp.stack(mv_p_l))
```
